```python
import math
import jax, jax.numpy as jnp
from jax import lax
import numpy as np

D_MODEL = 1024
BATCH = 16
SEQ = 2048
DEPTH = 2

N_META = 16
BLOCK = 128
PAD_FRONT = BLOCK - N_META

ATT_HEADS = 8
ATT_KV_HEADS = 2
ATT_HEAD_DIM = 64
ATT_WIDTH = ATT_HEADS * ATT_HEAD_DIM
ATT_KV_WIDTH = ATT_KV_HEADS * ATT_HEAD_DIM
WINDOW = 128
N_BUCKETS = 32
MAX_EXACT = N_BUCKETS // 2
MAX_DISTANCE = 128

RET_HEADS = 4
RET_HEAD_DIM = 128
RET_WIDTH = RET_HEADS * RET_HEAD_DIM
ROT_BASE = 10000.0

CONV_WIDTH = 512
CONV_K = 3

N_BRANCH = 3
BRANCH_WIDTH = 512
SPLITS = (ATT_WIDTH, ATT_KV_WIDTH, ATT_KV_WIDTH, ATT_WIDTH,
          RET_WIDTH, RET_WIDTH, RET_WIDTH, RET_WIDTH,
          CONV_WIDTH, CONV_WIDTH, CONV_WIDTH, CONV_WIDTH,
          N_BRANCH * D_MODEL)
PROJ_WIDTH = 8448
RMS_EPS = 1e-6
GN_EPS = 1e-6
NEG_INF = -1e30

kernel_name = "hybrid_gated_swa_retention_shortconv"


def _split_points():
    return [int(v) for v in np.cumsum(SPLITS)[:-1]]


def rms_norm(x, g):
    xf = x.astype(jnp.float32)
    y = xf * lax.rsqrt(jnp.mean(xf * xf, axis=-1, keepdims=True) + RMS_EPS)
    return (y * g.astype(jnp.float32)).astype(x.dtype)


def t5_causal_bucket(dist):
    n = jnp.maximum(dist, 0)
    nf = jnp.maximum(n, 1).astype(jnp.float32)
    large = MAX_EXACT + (jnp.log(nf / MAX_EXACT) / math.log(MAX_DISTANCE / MAX_EXACT)
                         * (N_BUCKETS - MAX_EXACT)).astype(jnp.int32)
    large = jnp.minimum(large, N_BUCKETS - 1)
    return jnp.where(n < MAX_EXACT, n, large)


def sliding_window_attention(q, k, v, sinks, rel_bias, valid):
    B, Lp = q.shape[0], q.shape[1]
    nc = Lp // BLOCK
    G = ATT_HEADS // ATT_KV_HEADS
    qb = q.reshape(B, nc, BLOCK, ATT_KV_HEADS, G, ATT_HEAD_DIM)

    def band(t):
        tb = t.reshape((B, nc, BLOCK) + t.shape[2:])
        prev = jnp.concatenate([jnp.zeros_like(tb[:, :1]), tb[:, :-1]], axis=1)
        return jnp.concatenate([prev, tb], axis=2)

    kb, vb = band(k), band(v)
    vblk = valid.reshape(nc, BLOCK)
    vprev = jnp.concatenate([jnp.zeros_like(vblk[:1]), vblk[:-1]], axis=0)
    valid_band = jnp.concatenate([vprev, vblk], axis=1)

    r = jnp.arange(BLOCK)[:, None]
    c = jnp.arange(2 * BLOCK)[None, :]
    dist = BLOCK + r - c
    in_window = (dist >= 0) & (dist < WINDOW)
    bias = rel_bias[t5_causal_bucket(dist)]
    bias = bias.reshape(BLOCK, 2 * BLOCK, ATT_KV_HEADS, G).transpose(2, 3, 0, 1).astype(jnp.float32)
    mask = in_window[None] & valid_band[:, None, :]

    s = jnp.einsum('bnqhgd,bnkhd->bnhgqk', qb, kb).astype(jnp.float32) * (ATT_HEAD_DIM ** -0.5) + bias
    s = jnp.where(mask[None, :, None, None], s, NEG_INF)
    sink = sinks.astype(jnp.float32).reshape(ATT_KV_HEADS, G)[None, None, :, :, None, None]
    m = jnp.maximum(jnp.max(s, axis=-1, keepdims=True), sink)
    p = jnp.exp(s - m)
    denom = jnp.sum(p, axis=-1, keepdims=True) + jnp.exp(sink - m)
    p = (p / denom).astype(v.dtype)
    o = jnp.einsum('bnhgqk,bnkhd->bnqhgd', p, vb)
    return o.reshape(B, Lp, ATT_WIDTH)


def rotate(t, pos):
    half = t.shape[-1] // 2
    theta = 1.0 / (ROT_BASE ** jnp.linspace(0.0, 1.0, half, dtype=jnp.float32))
    ang = pos.astype(jnp.float32)[:, None] * theta[None, :]
    cos = jnp.cos(ang)[None, :, None, :]
    sin = jnp.sin(ang)[None, :, None, :]
    t1, t2 = t[..., :half].astype(jnp.float32), t[..., half:].astype(jnp.float32)
    return jnp.concatenate([t1 * cos - t2 * sin, t1 * sin + t2 * cos], axis=-1).astype(t.dtype)


def retention(q, k, v, valid, pos):
    B, Lp = q.shape[0], q.shape[1]
    nc = Lp // BLOCK
    q = rotate(q, pos)
    k = rotate(k, pos) * (RET_HEAD_DIM ** -0.5)
    k = k * valid[None, :, None, None].astype(k.dtype)
    log_gamma = jnp.log1p(-(2.0 ** (-5.0 - jnp.arange(RET_HEADS, dtype=jnp.float32))))
    i = jnp.arange(BLOCK, dtype=jnp.float32)
    diff = i[:, None] - i[None, :]
    decay = jnp.where(diff[None] >= 0, jnp.exp(diff[None] * log_gamma[:, None, None]), 0.0)
    zeta = jnp.exp((BLOCK - 1 - i)[None, :] * log_gamma[:, None])
    xi = jnp.exp((i + 1)[None, :] * log_gamma[:, None])
    gamma_chunk = jnp.exp(BLOCK * log_gamma)[None, :, None, None]

    shp = (B, nc, BLOCK, RET_HEADS, RET_HEAD_DIM)
    qc, kc, vc = q.reshape(shp), k.reshape(shp), v.reshape(shp)
    inner_s = jnp.einsum('bnihd,bnjhd->bnhij', qc, kc) * decay
    inner = jnp.einsum('bnhij,bnjhe->bnihe', inner_s, vc)
    chunk_kv = jnp.einsum('bnjhd,bnjhe,hj->nbhde', kc, vc, zeta)

    def step(state, kv):
        return gamma_chunk * state + kv, state

    _, prev_states = lax.scan(step, jnp.zeros_like(chunk_kv[0]), chunk_kv)
    cross = jnp.einsum('bnihd,nbhde,hi->bnihe', qc, prev_states, xi)
    o = (inner + cross).astype(jnp.float32)
    mu = jnp.mean(o, axis=-1, keepdims=True)
    var = jnp.mean(jnp.square(o - mu), axis=-1, keepdims=True)
    o = (o - mu) * lax.rsqrt(var + GN_EPS)
    return o.reshape(B, Lp, RET_WIDTH).astype(q.dtype)


def short_conv_mixer(b_gate, c_gate, x_in, conv_w, valid):
    u = c_gate * x_in * valid[None, :, None].astype(x_in.dtype)
    y = lax.conv_general_dilated(u, conv_w[:, None, :].astype(u.dtype), window_strides=(1,),
                                 padding=[(CONV_K - 1, 0)],
                                 dimension_numbers=('NWC', 'WIO', 'NWC'),
                                 feature_group_count=CONV_WIDTH)
    return b_gate * y


def hybrid_layer(x, valid, pos, rel_bias, g_pre, w_in, conv_w, sinks, w_branch, w_out, g_post):
    B, Lp, _ = x.shape
    h = rms_norm(x, g_pre)
    proj = h @ w_in
    (aq, ak, av, ag, rq, rk, rv, rg, cb, cc, cx, cg, merge) = jnp.split(proj, _split_points(), axis=-1)

    ya = sliding_window_attention(aq.reshape(B, Lp, ATT_HEADS, ATT_HEAD_DIM),
                                  ak.reshape(B, Lp, ATT_KV_HEADS, ATT_HEAD_DIM),
                                  av.reshape(B, Lp, ATT_KV_HEADS, ATT_HEAD_DIM),
                                  sinks, rel_bias, valid) * jax.nn.silu(ag)
    yr = retention(rq.reshape(B, Lp, RET_HEADS, RET_HEAD_DIM),
                   rk.reshape(B, Lp, RET_HEADS, RET_HEAD_DIM),
                   rv.reshape(B, Lp, RET_HEADS, RET_HEAD_DIM), valid, pos) * jax.nn.silu(rg)
    yc = short_conv_mixer(cb, cc, cx, conv_w, valid) * jax.nn.silu(cg)

    branches = jnp.stack([ya, yr, yc], axis=2)
    branch_out = jnp.einsum('blgc,gcd->blgd', branches, w_branch)
    gates = jax.nn.sigmoid(merge.reshape(B, Lp, N_BRANCH, D_MODEL))
    mixed = jnp.sum(gates * branch_out, axis=2) @ w_out
    return x + rms_norm(mixed, g_post).astype(x.dtype)


def setup_inputs(seed: int = 0) -> dict:
    key = jax.random.key(seed)
    ks = jax.random.split(key, 10)
    f32 = jnp.float32
    x = jax.random.normal(ks[0], (BATCH, SEQ, D_MODEL), f32)
    meta_tokens = jax.random.normal(ks[1], (N_META, D_MODEL), f32)
    rel_bias = 0.1 * jax.random.normal(ks[2], (N_BUCKETS, ATT_HEADS), f32)
    norm_pre = 1.0 + 0.01 * jax.random.normal(ks[3], (DEPTH, D_MODEL), f32)
    w_in = jax.random.normal(ks[4], (DEPTH, D_MODEL, PROJ_WIDTH), f32) * (D_MODEL ** -0.5)
    conv_w = jax.random.normal(ks[5], (DEPTH, CONV_K, CONV_WIDTH), f32) * (CONV_K ** -0.5)
    attn_sinks = 0.5 * jax.random.normal(ks[6], (DEPTH, ATT_HEADS), f32)
    w_branch = jax.random.normal(ks[7], (DEPTH, N_BRANCH, BRANCH_WIDTH, D_MODEL), f32) * (BRANCH_WIDTH ** -0.5)
    w_out = jax.random.normal(ks[8], (DEPTH, D_MODEL, D_MODEL), f32) * (D_MODEL ** -0.5)
    norm_post = 1.0 + 0.01 * jax.random.normal(ks[9], (DEPTH, D_MODEL), f32)
    return {"x": x, "meta_tokens": meta_tokens, "rel_bias": rel_bias, "norm_pre": norm_pre,
            "w_in": w_in, "conv_w": conv_w, "attn_sinks": attn_sinks, "w_branch": w_branch,
            "w_out": w_out, "norm_post": norm_post}


def reference(x, meta_tokens, rel_bias, norm_pre, w_in, conv_w, attn_sinks, w_branch, w_out, norm_post):
    B, S, _ = x.shape
    pad = jnp.zeros((B, PAD_FRONT, D_MODEL), x.dtype)
    meta = jnp.broadcast_to(meta_tokens[None].astype(x.dtype), (B, N_META, D_MODEL))
    h = jnp.concatenate([pad, meta, x], axis=1)
    idx = jnp.arange(PAD_FRONT + N_META + S)
    valid = idx >= PAD_FRONT
    pos = idx - PAD_FRONT
    for l in range(DEPTH):
        h = hybrid_layer(h, valid, pos, rel_bias, norm_pre[l], w_in[l], conv_w[l],
                         attn_sinks[l], w_branch[l], w_out[l], norm_post[l])
    return h[:, PAD_FRONT + N_META:]
```

```python
import functools
import math

import numpy as np
import jax
import jax.numpy as jnp
from jax import lax
from jax.experimental import pallas as pl
from jax.experimental.pallas import tpu as pltpu

D_MODEL = 1024
DEPTH = 2
N_META = 16
BLOCK = 128
PAD_FRONT = BLOCK - N_META

ATT_HEADS = 8
ATT_KV_HEADS = 2
ATT_GROUP = ATT_HEADS // ATT_KV_HEADS
ATT_HEAD_DIM = 64
N_BUCKETS = 32
MAX_EXACT = N_BUCKETS // 2
MAX_DISTANCE = 128

RET_HEADS = 4
RET_HEAD_DIM = 128
ROT_BASE = 10000.0

CONV_K = 3
WIDTH = 512
N_BRANCH = 3
RMS_EPS = 1e-6
GN_EPS = 1e-6
NEG_INF = -1e30

_SPLITS = (512, 128, 128, 512, 512, 512, 512, 512, 512, 512, 512, 512, N_BRANCH * D_MODEL)
_OFF = [0] + [int(v) for v in np.cumsum(_SPLITS)]
(O_AQ, O_AK, O_AV, O_AG, O_RQ, O_RK, O_RV, O_RG, O_CB, O_CC, O_CX, O_CG, O_MERGE, PROJ_WIDTH) = _OFF

TILE = 256
CONV_HALO = 8
MERGE_CHUNK = 512
VMEM_LIMIT_BYTES = 56 * 1024 * 1024

_LOG_GAMMA = np.log1p(-(np.float32(2.0) ** (-5.0 - np.arange(RET_HEADS, dtype=np.float32)))).astype(np.float32)
_GAMMA_CHUNK = [float(v) for v in np.exp(np.float32(BLOCK) * _LOG_GAMMA).astype(np.float32)]


def _sigmoid(v):
    return 1.0 / (1.0 + jnp.exp(-v))


def _silu(v):
    return v * _sigmoid(v)


def _dot(a, b):
    return jnp.dot(a, b, preferred_element_type=jnp.float32)


def _dot_nt(a, b):
    return lax.dot_general(a, b, (((1,), (1,)), ((), ())), preferred_element_type=jnp.float32)


def _dot_tn(a, b):
    return lax.dot_general(a, b, (((0,), (0,)), ((), ())), preferred_element_type=jnp.float32)


def _layer_kernel(tile, has_pad, emit_state,
                  x_ref, cos_ref, sin_ref, gpre_ref, gpost_ref, win_ref, wbr_ref, wout_ref,
                  convw_ref, bias_ref, sink_ref, decay_ref, xi_ref, zeta_ref,
                  k0_ref, v0_ref, kmask_ref, s0_ref, u0_ref,
                  *rest):
    if emit_state:
        out_ref, kst_ref, vst_ref, sst_ref, ust_ref = rest[:5]
        rest = rest[5:]
    else:
        out_ref = rest[0]
        rest = rest[1:]
    (hb_s, q_s, kbuf, vbuf, ya_s, rq_s, qx_s, rk_s, kz_s, rv_s, yr_s, ubuf, state_s, y_s, mixed_s) = rest

    bf16 = jnp.bfloat16
    f32 = jnp.float32
    nb = tile // BLOCK
    t = pl.program_id(1)

    @pl.when(t == 0)
    def _():
        kbuf[0:BLOCK, :] = k0_ref[...]
        vbuf[0:BLOCK, :] = v0_ref[...]
        state_s[...] = s0_ref[...]
        ubuf[0:CONV_HALO, :] = u0_ref[...]

    x = x_ref[...]
    ms = jnp.mean(x * x, axis=-1, keepdims=True)
    hb_s[...] = (x * lax.rsqrt(ms + RMS_EPS) * gpre_ref[...]).astype(bf16)

    def proj(a, b):
        return _dot(hb_s[...], win_ref[:, a:b])

    if has_pad:
        row = lax.broadcasted_iota(jnp.int32, (tile, 1), 0)
        valid = (row >= PAD_FRONT).astype(f32)

    q_s[...] = (proj(O_AQ, O_AK) * (ATT_HEAD_DIM ** -0.5)).astype(bf16)
    kbuf[BLOCK:BLOCK + tile, :] = proj(O_AK, O_AV).astype(bf16)
    vbuf[BLOCK:BLOCK + tile, :] = proj(O_AV, O_AG).astype(bf16)

    def attn_block(r0, first_mask):
        for kvh in range(ATT_KV_HEADS):
            c0 = kvh * ATT_HEAD_DIM
            qs = jnp.concatenate(
                [q_s[pl.ds(r0, BLOCK), (kvh * ATT_GROUP + g) * ATT_HEAD_DIM:(kvh * ATT_GROUP + g + 1) * ATT_HEAD_DIM]
                 for g in range(ATT_GROUP)], axis=0)
            kb = kbuf[pl.ds(r0, 2 * BLOCK), c0:c0 + ATT_HEAD_DIM]
            vb = vbuf[pl.ds(r0, 2 * BLOCK), c0:c0 + ATT_HEAD_DIM]
            s = _dot_nt(qs, kb) + bias_ref[kvh]
            ps = []
            for g in range(ATT_GROUP):
                sg = s[g * BLOCK:(g + 1) * BLOCK]
                if first_mask is not None:
                    sg = sg + first_mask
                sink = sink_ref[kvh * ATT_GROUP + g]
                m = jnp.maximum(jnp.max(sg, axis=-1, keepdims=True), sink)
                p = jnp.exp(sg - m)
                denom = jnp.sum(p, axis=-1, keepdims=True) + jnp.exp(sink - m)
                ps.append((p * (1.0 / denom)).astype(bf16))
            o = _dot(jnp.concatenate(ps, axis=0), vb)
            for g in range(ATT_GROUP):
                h0 = (kvh * ATT_GROUP + g) * ATT_HEAD_DIM
                ya_s[pl.ds(r0, BLOCK), h0:h0 + ATT_HEAD_DIM] = o[g * BLOCK:(g + 1) * BLOCK]

    km = jnp.where(t == 0, kmask_ref[...], 0.0)
    if has_pad:
        col = lax.broadcasted_iota(jnp.int32, (1, BLOCK), 1)
        km_cur = jnp.where(col >= PAD_FRONT, 0.0, NEG_INF).astype(f32)
    else:
        km_cur = jnp.zeros_like(km)
    attn_block(0, jnp.concatenate([km, km_cur], axis=1))
    if nb > 1:
        def attn_body(j, carry):
            attn_block(pl.multiple_of(j * BLOCK, BLOCK), None)
            return carry
        lax.fori_loop(1, nb, attn_body, 0)

    y_s[0] = (ya_s[...] * _silu(proj(O_AG, O_RQ))).astype(bf16)

    cos_t = cos_ref[...]
    sin_t = sin_ref[...]

    def rotate(v):
        return v * cos_t + pltpu.roll(v, RET_HEAD_DIM // 2, 1) * sin_t

    rq = proj(O_RQ, O_RK)
    rk = proj(O_RK, O_RV)
    for h in range(RET_HEADS):
        hs = slice(h * RET_HEAD_DIM, (h + 1) * RET_HEAD_DIM)
        qr = rotate(rq[:, hs])
        rq_s[:, hs] = qr.astype(bf16)
        qx_s[:, hs] = (qr * xi_ref[:, hs]).astype(bf16)
        kr = rotate(rk[:, hs]) * (RET_HEAD_DIM ** -0.5)
        if has_pad:
            kr = kr * valid
        rk_s[:, hs] = kr.astype(bf16)
        kz_s[:, hs] = (kr * zeta_ref[:, hs]).astype(bf16)
    rv_s[...] = proj(O_RV, O_RG).astype(bf16)

    def ret_body(j, carry):
        r0 = pl.multiple_of(j * BLOCK, BLOCK)
        for h in range(RET_HEADS):
            hs = slice(h * RET_HEAD_DIM, (h + 1) * RET_HEAD_DIM)
            qh = rq_s[pl.ds(r0, BLOCK), hs]
            kh = rk_s[pl.ds(r0, BLOCK), hs]
            vh = rv_s[pl.ds(r0, BLOCK), hs]
            st = state_s[h]
            inner_s = _dot_nt(qh, kh) * decay_ref[h]
            lhs = jnp.concatenate([inner_s.astype(bf16), qx_s[pl.ds(r0, BLOCK), hs]], axis=1)
            rhs = jnp.concatenate([vh, st.astype(bf16)], axis=0)
            o = _dot(lhs, rhs)
            mu = jnp.mean(o, axis=-1, keepdims=True)
            d = o - mu
            var = jnp.mean(d * d, axis=-1, keepdims=True)
            yr_s[pl.ds(r0, BLOCK), hs] = d * lax.rsqrt(var + GN_EPS)
            state_s[h] = _GAMMA_CHUNK[h] * st + _dot_tn(kz_s[pl.ds(r0, BLOCK), hs], vh)
        return carry

    lax.fori_loop(0, nb, ret_body, 0)
    y_s[1] = (yr_s[...] * _silu(proj(O_RG, O_CB))).astype(bf16)

    u = proj(O_CC, O_CX) * proj(O_CX, O_CG)
    if has_pad:
        u = u * valid
    ubuf[CONV_HALO:CONV_HALO + tile, :] = u
    conv = (convw_ref[0:1, :] * ubuf[CONV_HALO - 2:CONV_HALO - 2 + tile, :]
            + convw_ref[1:2, :] * ubuf[CONV_HALO - 1:CONV_HALO - 1 + tile, :]
            + convw_ref[2:3, :] * u)
    y_s[2] = (proj(O_CB, O_CC) * conv * _silu(proj(O_CG, O_MERGE))).astype(bf16)

    for n0 in range(0, D_MODEL, MERGE_CHUNK):
        acc = None
        for g in range(N_BRANCH):
            c0 = O_MERGE + g * D_MODEL + n0
            gate = _sigmoid(proj(c0, c0 + MERGE_CHUNK))
            term = gate * _dot(y_s[g], wbr_ref[g, :, n0:n0 + MERGE_CHUNK])
            acc = term if acc is None else acc + term
        mixed_s[:, n0:n0 + MERGE_CHUNK] = acc.astype(bf16)

    o = _dot(mixed_s[...], wout_ref[...])
    oms = jnp.mean(o * o, axis=-1, keepdims=True)
    out_ref[...] = x_ref[...] + o * lax.rsqrt(oms + RMS_EPS) * gpost_ref[...]

    ubuf[0:CONV_HALO, :] = ubuf[tile:tile + CONV_HALO, :]
    kbuf[0:BLOCK, :] = kbuf[tile:tile + BLOCK, :]
    vbuf[0:BLOCK, :] = vbuf[tile:tile + BLOCK, :]
    if emit_state:
        kst_ref[...] = kbuf[0:BLOCK, :]
        vst_ref[...] = vbuf[0:BLOCK, :]
        sst_ref[...] = state_s[...]
        ust_ref[...] = ubuf[0:CONV_HALO, :]


def _const_spec(shape):
    return pl.BlockSpec(shape, lambda b, t: (0,) * len(shape), pipeline_mode=pl.Buffered(1))


def _layer_call(x, cos_tab, sin_tab, consts, state, *, tile, has_pad, emit_state):
    B, L, _ = x.shape
    assert L % tile == 0 and tile % BLOCK == 0
    (gpre, gpost, win, wbr, wout, convw, bias_tab, sinks, decay, xi_tab, zeta_tab) = consts
    k0, v0, kmask, s0, u0 = state
    nb = tile // BLOCK
    xi_t = jnp.tile(xi_tab, (nb, 1))
    zeta_t = jnp.tile(zeta_tab, (nb, 1))

    in_specs = [
        pl.BlockSpec((None, tile, D_MODEL), lambda b, t: (b, t, 0)),
        pl.BlockSpec((tile, RET_HEAD_DIM), lambda b, t: (t, 0)),
        pl.BlockSpec((tile, RET_HEAD_DIM), lambda b, t: (t, 0)),
        _const_spec((1, D_MODEL)), _const_spec((1, D_MODEL)),
        _const_spec((D_MODEL, PROJ_WIDTH)),
        _const_spec((N_BRANCH, WIDTH, D_MODEL)),
        _const_spec((D_MODEL, D_MODEL)),
        _const_spec((CONV_K, WIDTH)),
        _const_spec((ATT_KV_HEADS, ATT_GROUP * BLOCK, 2 * BLOCK)),
        pl.BlockSpec(memory_space=pltpu.SMEM),
        _const_spec((RET_HEADS, BLOCK, BLOCK)),
        _const_spec((tile, WIDTH)), _const_spec((tile, WIDTH)),
        _const_spec((BLOCK, BLOCK)), _const_spec((BLOCK, BLOCK)),
        _const_spec((1, BLOCK)),
        _const_spec((RET_HEADS, BLOCK, BLOCK)),
        _const_spec((CONV_HALO, WIDTH)),
    ]
    out_shape = [jax.ShapeDtypeStruct((B, L, D_MODEL), jnp.float32)]
    out_specs = [pl.BlockSpec((None, tile, D_MODEL), lambda b, t: (b, t, 0))]
    if emit_state:
        out_shape += [jax.ShapeDtypeStruct((BLOCK, BLOCK), jnp.bfloat16),
                      jax.ShapeDtypeStruct((BLOCK, BLOCK), jnp.bfloat16),
                      jax.ShapeDtypeStruct((RET_HEADS, BLOCK, BLOCK), jnp.float32),
                      jax.ShapeDtypeStruct((CONV_HALO, WIDTH), jnp.float32)]
        out_specs += [pl.BlockSpec((BLOCK, BLOCK), lambda b, t: (0, 0)),
                      pl.BlockSpec((BLOCK, BLOCK), lambda b, t: (0, 0)),
                      pl.BlockSpec((RET_HEADS, BLOCK, BLOCK), lambda b, t: (0, 0, 0)),
                      pl.BlockSpec((CONV_HALO, WIDTH), lambda b, t: (0, 0))]
    bf16, f32 = jnp.bfloat16, jnp.float32
    scratch = [
        pltpu.VMEM((tile, D_MODEL), bf16),
        pltpu.VMEM((tile, WIDTH), bf16),
        pltpu.VMEM((tile + BLOCK, BLOCK), bf16),
        pltpu.VMEM((tile + BLOCK, BLOCK), bf16),
        pltpu.VMEM((tile, WIDTH), f32),
        pltpu.VMEM((tile, WIDTH), bf16),
        pltpu.VMEM((tile, WIDTH), bf16),
        pltpu.VMEM((tile, WIDTH), bf16),
        pltpu.VMEM((tile, WIDTH), bf16),
        pltpu.VMEM((tile, WIDTH), bf16),
        pltpu.VMEM((tile, WIDTH), f32),
        pltpu.VMEM((tile + CONV_HALO, WIDTH), f32),
        pltpu.VMEM((RET_HEADS, BLOCK, BLOCK), f32),
        pltpu.VMEM((N_BRANCH, tile, WIDTH), bf16),
        pltpu.VMEM((tile, D_MODEL), bf16),
    ]
    res = pl.pallas_call(
        functools.partial(_layer_kernel, tile, has_pad, emit_state),
        grid=(B, L // tile),
        in_specs=in_specs,
        out_specs=out_specs,
        out_shape=out_shape,
        scratch_shapes=scratch,
        compiler_params=pltpu.CompilerParams(
            dimension_semantics=("arbitrary", "arbitrary"),
            vmem_limit_bytes=VMEM_LIMIT_BYTES),
        name="hybrid_layer_prefix" if emit_state else "hybrid_layer_main",
    )(x, cos_tab, sin_tab, gpre, gpost, win, wbr, wout, convw, bias_tab, sinks, decay, xi_t, zeta_t,
      k0, v0, kmask, s0, u0)
    return res


def _t5_bucket_table():
    r = np.arange(BLOCK)[:, None]
    c = np.arange(2 * BLOCK)[None, :]
    dist = BLOCK + r - c
    n = np.maximum(dist, 0)
    nf = np.maximum(n, 1).astype(np.float32)
    large = MAX_EXACT + (np.log(nf / np.float32(MAX_EXACT)) / np.float32(math.log(MAX_DISTANCE / MAX_EXACT))
                         * np.float32(N_BUCKETS - MAX_EXACT)).astype(np.int32)
    large = np.minimum(large, N_BUCKETS - 1)
    bucket = np.where(n < MAX_EXACT, n, large).astype(np.int32)
    in_window = (dist >= 0) & (dist < BLOCK)
    return bucket, in_window


def _position_tables(n_pos):
    half = RET_HEAD_DIM // 2
    theta = 1.0 / (ROT_BASE ** jnp.linspace(0.0, 1.0, half, dtype=jnp.float32))
    ang = jnp.arange(n_pos).astype(jnp.float32)[:, None] * theta[None, :]
    cos, sin = jnp.cos(ang), jnp.sin(ang)
    return jnp.concatenate([cos, cos], axis=1), jnp.concatenate([-sin, sin], axis=1)


def _retention_tables():
    log_gamma = jnp.asarray(_LOG_GAMMA)
    i = jnp.arange(BLOCK, dtype=jnp.float32)
    diff = i[:, None] - i[None, :]
    decay = jnp.where(diff[None] >= 0, jnp.exp(diff[None] * log_gamma[:, None, None]), 0.0)
    zeta = jnp.exp((BLOCK - 1 - i)[None, :] * log_gamma[:, None])
    xi = jnp.exp((i + 1)[None, :] * log_gamma[:, None])
    expand = lambda a: jnp.repeat(a.T, RET_HEAD_DIM, axis=1)
    return decay, expand(xi), expand(zeta)


def kernel(x, meta_tokens, rel_bias, norm_pre, w_in, conv_w, attn_sinks, w_branch, w_out, norm_post):
    B, S, _ = x.shape
    f32, bf16 = jnp.float32, jnp.bfloat16

    bucket, in_window = _t5_bucket_table()
    bias = jnp.transpose(rel_bias.astype(f32)[bucket], (2, 0, 1))
    bias = jnp.where(in_window[None], bias, NEG_INF)
    bias_tab = bias.reshape(ATT_KV_HEADS, ATT_GROUP * BLOCK, 2 * BLOCK)
    cos_tab, sin_tab = _position_tables(N_META + S)
    cos_pre = jnp.concatenate([jnp.ones((PAD_FRONT, RET_HEAD_DIM), f32), cos_tab[:N_META]], axis=0)
    sin_pre = jnp.concatenate([jnp.zeros((PAD_FRONT, RET_HEAD_DIM), f32), sin_tab[:N_META]], axis=0)
    decay, xi_tab, zeta_tab = _retention_tables()

    zero_state = (jnp.zeros((BLOCK, BLOCK), bf16), jnp.zeros((BLOCK, BLOCK), bf16),
                  jnp.full((1, BLOCK), NEG_INF, f32),
                  jnp.zeros((RET_HEADS, BLOCK, BLOCK), f32), jnp.zeros((CONV_HALO, WIDTH), f32))
    pad_mask = jnp.where(jnp.arange(BLOCK) >= PAD_FRONT, 0.0, NEG_INF).astype(f32)[None, :]

    h_pre = jnp.concatenate([jnp.zeros((PAD_FRONT, D_MODEL), f32), meta_tokens.astype(f32)], axis=0)[None]
    h = x
    for l in range(DEPTH):
        consts = (norm_pre[l][None, :], norm_post[l][None, :], w_in[l].astype(bf16), w_branch[l].astype(bf16),
                  w_out[l].astype(bf16), conv_w[l], bias_tab, attn_sinks[l], decay, xi_tab, zeta_tab)
        h_pre, k0, v0, s0, u0 = _layer_call(h_pre, cos_pre, sin_pre, consts, zero_state,
                                            tile=BLOCK, has_pad=True, emit_state=True)
        (h,) = _layer_call(h, cos_tab[N_META:], sin_tab[N_META:], consts, (k0, v0, pad_mask, s0, u0),
                           tile=TILE, has_pad=False, emit_state=False)
    return h
```

```python
import functools
import math

import numpy as np
import jax
import jax.numpy as jnp
from jax import lax
from jax.experimental import pallas as pl
from jax.experimental.pallas import tpu as pltpu

D_MODEL = 1024
DEPTH = 2
N_META = 16
BLOCK = 128
PAD_FRONT = BLOCK - N_META

ATT_HEADS = 8
ATT_KV_HEADS = 2
ATT_GROUP = ATT_HEADS // ATT_KV_HEADS
ATT_HEAD_DIM = 64
N_BUCKETS = 32
MAX_EXACT = N_BUCKETS // 2
MAX_DISTANCE = 128

RET_HEADS = 4
RET_HEAD_DIM = 128
ROT_BASE = 10000.0

CONV_K = 3
WIDTH = 512
N_BRANCH = 3
RMS_EPS = 1e-6
GN_EPS = 1e-6
NEG_INF = -1e30

_SPLITS = (512, 128, 128, 512, 512, 512, 512, 512, 512, 512, 512, 512, N_BRANCH * D_MODEL)
_OFF = [0] + [int(v) for v in np.cumsum(_SPLITS)]
(O_AQ, O_AK, O_AV, O_AG, O_RQ, O_RK, O_RV, O_RG, O_CB, O_CC, O_CX, O_CG, O_MERGE, PROJ_WIDTH) = _OFF

TILE = 256
CONV_HALO = 8
MERGE_CHUNK = 512
VMEM_LIMIT_BYTES = 56 * 1024 * 1024

_LOG_GAMMA = np.log1p(-(np.float32(2.0) ** (-5.0 - np.arange(RET_HEADS, dtype=np.float32)))).astype(np.float32)
_GAMMA_CHUNK = [float(v) for v in np.exp(np.float32(BLOCK) * _LOG_GAMMA).astype(np.float32)]


def _sigmoid(v):
    return 1.0 / (1.0 + jnp.exp(-v))


def _silu(v):
    return v * _sigmoid(v)


def _dot(a, b):
    return jnp.dot(a, b, preferred_element_type=jnp.float32)


def _dot_nt(a, b):
    return lax.dot_general(a, b, (((1,), (1,)), ((), ())), preferred_element_type=jnp.float32)


def _dot_tn(a, b):
    return lax.dot_general(a, b, (((0,), (0,)), ((), ())), preferred_element_type=jnp.float32)


def _layer_kernel(layer, tile, is_prefix, *refs):
    (x_ref, cos_ref, sin_ref, gpre_ref, gpost_ref, win_ref, wbr_ref, wout_ref,
     convw_ref, sink_ref, decay_ref, xi_ref, zeta_ref) = refs[:13]
    refs = refs[13:]
    if is_prefix:
        bucket_ref, relb_ref = refs[:2]
        out_ref, bias_ref, kst_ref, vst_ref, sst_ref, ust_ref = refs[2:8]
        refs = refs[8:]
    else:
        bias_ref, k0_ref, v0_ref, s0_ref, u0_ref = refs[:5]
        out_ref = refs[5]
        refs = refs[6:]
    (hb_s, q_s, kbuf, vbuf, ya_s, rq_s, qx_s, rk_s, kz_s, rv_s, yr_s, ubuf, state_s, y_s, mixed_s) = refs

    bf16 = jnp.bfloat16
    f32 = jnp.float32
    nb = tile // BLOCK
    t = pl.program_id(1)

    if is_prefix:
        kbuf[0:BLOCK, :] = jnp.zeros((BLOCK, BLOCK), bf16)
        vbuf[0:BLOCK, :] = jnp.zeros((BLOCK, BLOCK), bf16)
        state_s[...] = jnp.zeros_like(state_s)
        ubuf[0:CONV_HALO, :] = jnp.zeros((CONV_HALO, WIDTH), f32)
        bkt = bucket_ref[...]
        for hq in range(ATT_HEADS):
            acc = jnp.zeros((BLOCK, BLOCK), f32)
            for b in range(N_BUCKETS):
                acc = jnp.where(bkt == b, relb_ref[b, hq], acc)
            bias_ref[hq // ATT_GROUP, (hq % ATT_GROUP) * BLOCK:(hq % ATT_GROUP + 1) * BLOCK, :] = acc
        row = lax.broadcasted_iota(jnp.int32, (tile, 1), 0)
        valid = (row >= PAD_FRONT).astype(f32)
    else:
        @pl.when(t == 0)
        def _():
            kbuf[0:BLOCK, :] = k0_ref[...]
            vbuf[0:BLOCK, :] = v0_ref[...]
            state_s[...] = s0_ref[...]
            ubuf[0:CONV_HALO, :] = u0_ref[...]

    qrow = lax.broadcasted_iota(jnp.int32, (ATT_GROUP * BLOCK, BLOCK), 0) & (BLOCK - 1)
    kcol = lax.broadcasted_iota(jnp.int32, (ATT_GROUP * BLOCK, BLOCK), 1)
    from_cur = kcol <= qrow
    if is_prefix:
        first_mask = jnp.where(from_cur & (kcol >= PAD_FRONT), 0.0, NEG_INF).astype(f32)
    else:
        first_mask = jnp.where((t == 0) & jnp.logical_not(from_cur) & (kcol < PAD_FRONT), NEG_INF, 0.0).astype(f32)

    x = x_ref[...]
    ms = jnp.mean(x * x, axis=-1, keepdims=True)
    hb_s[...] = (x * lax.rsqrt(ms + RMS_EPS) * gpre_ref[...]).astype(bf16)

    def proj(a, b):
        return _dot(hb_s[...], win_ref[:, a:b])

    q_s[...] = (proj(O_AQ, O_AK) * (ATT_HEAD_DIM ** -0.5)).astype(bf16)
    kbuf[BLOCK:BLOCK + tile, :] = proj(O_AK, O_AV).astype(bf16)
    vbuf[BLOCK:BLOCK + tile, :] = proj(O_AV, O_AG).astype(bf16)

    for j in range(nb):
        rows = slice(j * BLOCK, (j + 1) * BLOCK)
        band = slice(j * BLOCK, (j + 2) * BLOCK)
        for kvh in range(ATT_KV_HEADS):
            c0 = kvh * ATT_HEAD_DIM
            qs = jnp.concatenate(
                [q_s[rows, (kvh * ATT_GROUP + g) * ATT_HEAD_DIM:(kvh * ATT_GROUP + g + 1) * ATT_HEAD_DIM]
                 for g in range(ATT_GROUP)], axis=0)
            kb = kbuf[band, c0:c0 + ATT_HEAD_DIM]
            vb = vbuf[band, c0:c0 + ATT_HEAD_DIM]
            s2 = _dot_nt(qs, kb)
            s = jnp.where(from_cur, s2[:, BLOCK:], s2[:, :BLOCK]) + bias_ref[kvh]
            if j == 0:
                s = s + first_mask
            ps = []
            for g in range(ATT_GROUP):
                sg = s[g * BLOCK:(g + 1) * BLOCK]
                sink = sink_ref[layer, kvh * ATT_GROUP + g]
                m = jnp.maximum(jnp.max(sg, axis=-1, keepdims=True), sink)
                p = jnp.exp(sg - m)
                denom = jnp.sum(p, axis=-1, keepdims=True) + jnp.exp(sink - m)
                ps.append(p * (1.0 / denom))
            pn = jnp.concatenate(ps, axis=0)
            pband = jnp.concatenate([jnp.where(from_cur, 0.0, pn), jnp.where(from_cur, pn, 0.0)], axis=1)
            o = _dot(pband.astype(bf16), vb)
            for g in range(ATT_GROUP):
                h0 = (kvh * ATT_GROUP + g) * ATT_HEAD_DIM
                ya_s[rows, h0:h0 + ATT_HEAD_DIM] = o[g * BLOCK:(g + 1) * BLOCK]

    y_s[0] = (ya_s[...] * _silu(proj(O_AG, O_RQ))).astype(bf16)

    cos_t = cos_ref[...]
    sin_t = sin_ref[...]

    def rotate(v):
        return v * cos_t + pltpu.roll(v, RET_HEAD_DIM // 2, 1) * sin_t

    rq = proj(O_RQ, O_RK)
    rk = proj(O_RK, O_RV)
    for h in range(RET_HEADS):
        hs = slice(h * RET_HEAD_DIM, (h + 1) * RET_HEAD_DIM)
        qr = rotate(rq[:, hs])
        rq_s[:, hs] = qr.astype(bf16)
        qx_s[:, hs] = (qr * xi_ref[:, hs]).astype(bf16)
        kr = rotate(rk[:, hs]) * (RET_HEAD_DIM ** -0.5)
        if is_prefix:
            kr = kr * valid
        rk_s[:, hs] = kr.astype(bf16)
        kz_s[:, hs] = (kr * zeta_ref[:, hs]).astype(bf16)
    rv_s[...] = proj(O_RV, O_RG).astype(bf16)

    heads = [slice(h * RET_HEAD_DIM, (h + 1) * RET_HEAD_DIM) for h in range(RET_HEADS)]
    blocks = [slice(j * BLOCK, (j + 1) * BLOCK) for j in range(nb)]
    scores = [[_dot_nt(rq_s[rows, hs], rk_s[rows, hs]) for rows in blocks] for hs in heads]
    chunk_kv = [[_dot_tn(kz_s[rows, hs], rv_s[rows, hs]) for rows in blocks] for hs in heads]
    outs = []
    for h, hs in enumerate(heads):
        st = state_s[h]
        for j, rows in enumerate(blocks):
            lhs = jnp.concatenate([(scores[h][j] * decay_ref[h]).astype(bf16), qx_s[rows, hs]], axis=1)
            rhs = jnp.concatenate([rv_s[rows, hs], st.astype(bf16)], axis=0)
            outs.append((rows, hs, _dot(lhs, rhs)))
            st = _GAMMA_CHUNK[h] * st + chunk_kv[h][j]
        state_s[h] = st
    for rows, hs, o in outs:
        mu = jnp.mean(o, axis=-1, keepdims=True)
        d = o - mu
        var = jnp.mean(d * d, axis=-1, keepdims=True)
        yr_s[rows, hs] = d * lax.rsqrt(var + GN_EPS)
    y_s[1] = (yr_s[...] * _silu(proj(O_RG, O_CB))).astype(bf16)

    u = proj(O_CC, O_CX) * proj(O_CX, O_CG)
    if is_prefix:
        u = u * valid
    ubuf[CONV_HALO:CONV_HALO + tile, :] = u
    conv = (convw_ref[0:1, :] * ubuf[CONV_HALO - 2:CONV_HALO - 2 + tile, :]
            + convw_ref[1:2, :] * ubuf[CONV_HALO - 1:CONV_HALO - 1 + tile, :]
            + convw_ref[2:3, :] * u)
    y_s[2] = (proj(O_CB, O_CC) * conv * _silu(proj(O_CG, O_MERGE))).astype(bf16)

    for n0 in range(0, D_MODEL, MERGE_CHUNK):
        acc = None
        for g in range(N_BRANCH):
            c0 = O_MERGE + g * D_MODEL + n0
            gate = _sigmoid(proj(c0, c0 + MERGE_CHUNK))
            term = gate * _dot(y_s[g], wbr_ref[g, :, n0:n0 + MERGE_CHUNK])
            acc = term if acc is None else acc + term
        mixed_s[:, n0:n0 + MERGE_CHUNK] = acc.astype(bf16)

    o = _dot(mixed_s[...], wout_ref[...])
    oms = jnp.mean(o * o, axis=-1, keepdims=True)
    out_ref[...] = x_ref[...] + o * lax.rsqrt(oms + RMS_EPS) * gpost_ref[...]

    ubuf[0:CONV_HALO, :] = ubuf[tile:tile + CONV_HALO, :]
    kbuf[0:BLOCK, :] = kbuf[tile:tile + BLOCK, :]
    vbuf[0:BLOCK, :] = vbuf[tile:tile + BLOCK, :]
    if is_prefix:
        kst_ref[...] = kbuf[0:BLOCK, :]
        vst_ref[...] = vbuf[0:BLOCK, :]
        sst_ref[...] = state_s[...]
        ust_ref[...] = ubuf[0:CONV_HALO, :]


def _const_spec(shape, index=None):
    index = (0,) * len(shape) if index is None else index
    return pl.BlockSpec(shape, lambda b, t: index, pipeline_mode=pl.Buffered(1))


def _layer_call(layer, x, cos_tab, sin_tab, params, tables, extra, *, tile, is_prefix):
    B, L, _ = x.shape
    assert L % tile == 0 and tile % BLOCK == 0
    gpre, gpost, win, wbr, wout, convw, sinks = params
    decay, xi_tab, zeta_tab = tables
    nb = tile // BLOCK
    xi_t = jnp.tile(xi_tab, (nb, 1))
    zeta_t = jnp.tile(zeta_tab, (nb, 1))
    bf16, f32 = jnp.bfloat16, jnp.float32
    bias_shape = (ATT_KV_HEADS, ATT_GROUP * BLOCK, BLOCK)

    in_specs = [
        pl.BlockSpec((None, tile, D_MODEL), lambda b, t: (b, t, 0)),
        pl.BlockSpec((tile, RET_HEAD_DIM), lambda b, t: (t, 0)),
        pl.BlockSpec((tile, RET_HEAD_DIM), lambda b, t: (t, 0)),
        _const_spec((None, 1, D_MODEL), (layer, 0, 0)),
        _const_spec((None, 1, D_MODEL), (layer, 0, 0)),
        _const_spec((None, D_MODEL, PROJ_WIDTH), (layer, 0, 0)),
        _const_spec((None, N_BRANCH, WIDTH, D_MODEL), (layer, 0, 0, 0)),
        _const_spec((None, D_MODEL, D_MODEL), (layer, 0, 0)),
        _const_spec((None, CONV_K, WIDTH), (layer, 0, 0)),
        pl.BlockSpec(memory_space=pltpu.SMEM),
        _const_spec((RET_HEADS, BLOCK, BLOCK)),
        _const_spec((tile, WIDTH)), _const_spec((tile, WIDTH)),
    ]
    out_shape = [jax.ShapeDtypeStruct((B, L, D_MODEL), f32)]
    out_specs = [pl.BlockSpec((None, tile, D_MODEL), lambda b, t: (b, t, 0))]
    state_shapes = [((BLOCK, BLOCK), bf16), ((BLOCK, BLOCK), bf16),
                    ((RET_HEADS, BLOCK, BLOCK), f32), ((CONV_HALO, WIDTH), f32)]
    if is_prefix:
        in_specs += [_const_spec((BLOCK, BLOCK)), pl.BlockSpec(memory_space=pltpu.SMEM)]
        for shape, dt in [(bias_shape, f32)] + state_shapes:
            out_shape.append(jax.ShapeDtypeStruct(shape, dt))
            out_specs.append(pl.BlockSpec(shape, lambda b, t, n=len(shape): (0,) * n))
    else:
        in_specs += [_const_spec(bias_shape), _const_spec((BLOCK, BLOCK)), _const_spec((BLOCK, BLOCK)),
                     _const_spec((RET_HEADS, BLOCK, BLOCK)),
                     _const_spec((CONV_HALO, WIDTH))]
    scratch = [
        pltpu.VMEM((tile, D_MODEL), bf16),
        pltpu.VMEM((tile, WIDTH), bf16),
        pltpu.VMEM((tile + BLOCK, BLOCK), bf16),
        pltpu.VMEM((tile + BLOCK, BLOCK), bf16),
        pltpu.VMEM((tile, WIDTH), f32),
        pltpu.VMEM((tile, WIDTH), bf16),
        pltpu.VMEM((tile, WIDTH), bf16),
        pltpu.VMEM((tile, WIDTH), bf16),
        pltpu.VMEM((tile, WIDTH), bf16),
        pltpu.VMEM((tile, WIDTH), bf16),
        pltpu.VMEM((tile, WIDTH), f32),
        pltpu.VMEM((tile + CONV_HALO, WIDTH), f32),
        pltpu.VMEM((RET_HEADS, BLOCK, BLOCK), f32),
        pltpu.VMEM((N_BRANCH, tile, WIDTH), bf16),
        pltpu.VMEM((tile, D_MODEL), bf16),
    ]
    return pl.pallas_call(
        functools.partial(_layer_kernel, layer, tile, is_prefix),
        grid=(B, L // tile),
        in_specs=in_specs,
        out_specs=out_specs,
        out_shape=out_shape,
        scratch_shapes=scratch,
        compiler_params=pltpu.CompilerParams(
            dimension_semantics=("arbitrary", "arbitrary"),
            vmem_limit_bytes=VMEM_LIMIT_BYTES),
        name="hybrid_layer_prefix" if is_prefix else "hybrid_layer_main",
    )(x, cos_tab, sin_tab, gpre, gpost, win, wbr, wout, convw, sinks, decay, xi_t, zeta_t, *extra)


def _t5_bucket_table():
    r = np.arange(BLOCK)[:, None]
    c = np.arange(BLOCK)[None, :]
    n = np.where(c <= r, r - c, BLOCK + r - c)
    nf = np.maximum(n, 1).astype(np.float32)
    large = MAX_EXACT + (np.log(nf / np.float32(MAX_EXACT)) / np.float32(math.log(MAX_DISTANCE / MAX_EXACT))
                         * np.float32(N_BUCKETS - MAX_EXACT)).astype(np.int32)
    large = np.minimum(large, N_BUCKETS - 1)
    return np.where(n < MAX_EXACT, n, large).astype(np.int32)


def _position_tables(n_pos):
    half = RET_HEAD_DIM // 2
    theta = 1.0 / (ROT_BASE ** jnp.linspace(0.0, 1.0, half, dtype=jnp.float32))
    ang = jnp.arange(n_pos).astype(jnp.float32)[:, None] * theta[None, :]
    cos, sin = jnp.cos(ang), jnp.sin(ang)
    return jnp.concatenate([cos, cos], axis=1), jnp.concatenate([-sin, sin], axis=1)


def _retention_tables():
    log_gamma = jnp.asarray(_LOG_GAMMA)
    i = jnp.arange(BLOCK, dtype=jnp.float32)
    diff = i[:, None] - i[None, :]
    decay = jnp.where(diff[None] >= 0, jnp.exp(diff[None] * log_gamma[:, None, None]), 0.0)
    zeta = jnp.exp((BLOCK - 1 - i)[None, :] * log_gamma[:, None])
    xi = jnp.exp((i + 1)[None, :] * log_gamma[:, None])
    expand = lambda a: jnp.repeat(a.T, RET_HEAD_DIM, axis=1)
    return decay, expand(xi), expand(zeta)


def kernel(x, meta_tokens, rel_bias, norm_pre, w_in, conv_w, attn_sinks, w_branch, w_out, norm_post):
    B, S, _ = x.shape
    f32, bf16 = jnp.float32, jnp.bfloat16

    bucket = jnp.asarray(_t5_bucket_table())
    cos_tab, sin_tab = _position_tables(N_META + S)
    cos_pre = jnp.concatenate([jnp.ones((PAD_FRONT, RET_HEAD_DIM), f32), cos_tab[:N_META]], axis=0)
    sin_pre = jnp.concatenate([jnp.zeros((PAD_FRONT, RET_HEAD_DIM), f32), sin_tab[:N_META]], axis=0)
    tables = _retention_tables()
    params = (norm_pre[:, None, :], norm_post[:, None, :], w_in.astype(bf16), w_branch.astype(bf16),
              w_out.astype(bf16), conv_w, attn_sinks)

    h_pre = jnp.concatenate([jnp.zeros((PAD_FRONT, D_MODEL), f32), meta_tokens.astype(f32)], axis=0)[None]
    h = x
    for l in range(DEPTH):
        h_pre, bias_tab, k0, v0, s0, u0 = _layer_call(
            l, h_pre, cos_pre, sin_pre, params, tables, (bucket, rel_bias), tile=BLOCK, is_prefix=True)
        (h,) = _layer_call(
            l, h, cos_tab[N_META:], sin_tab[N_META:], params, tables,
            (bias_tab, k0, v0, s0, u0), tile=TILE, is_prefix=False)
    return h
```

```python
import functools
import math

import numpy as np
import jax
import jax.numpy as jnp
from jax import lax
from jax.experimental import pallas as pl
from jax.experimental.pallas import tpu as pltpu

D_MODEL = 1024
DEPTH = 2
N_META = 16
BLOCK = 128
PAD_FRONT = BLOCK - N_META

ATT_HEADS = 8
ATT_KV_HEADS = 2
ATT_GROUP = ATT_HEADS // ATT_KV_HEADS
ATT_HEAD_DIM = 64
N_BUCKETS = 32
MAX_EXACT = N_BUCKETS // 2
MAX_DISTANCE = 128

RET_HEADS = 4
RET_HEAD_DIM = 128
ROT_BASE = 10000.0

CONV_K = 3
WIDTH = 512
N_BRANCH = 3
RMS_EPS = 1e-6
GN_EPS = 1e-6
NEG_INF = -1e30

_SPLITS = (512, 128, 128, 512, 512, 512, 512, 512, 512, 512, 512, 512, N_BRANCH * D_MODEL)
_OFF = [0] + [int(v) for v in np.cumsum(_SPLITS)]
(O_AQ, O_AK, O_AV, O_AG, O_RQ, O_RK, O_RV, O_RG, O_CB, O_CC, O_CX, O_CG, O_MERGE, PROJ_WIDTH) = _OFF

TILE = 512
CONV_HALO = 8
MERGE_CHUNK = 512
VMEM_LIMIT_BYTES = 56 * 1024 * 1024

_LOG_GAMMA = np.log1p(-(np.float32(2.0) ** (-5.0 - np.arange(RET_HEADS, dtype=np.float32)))).astype(np.float32)
_GAMMA_CHUNK = [float(v) for v in np.exp(np.float32(BLOCK) * _LOG_GAMMA).astype(np.float32)]


def _sigmoid(v):
    return 1.0 / (1.0 + jnp.exp(-v))


def _silu(v):
    return v * _sigmoid(v)


def _dot(a, b):
    return jnp.dot(a, b, preferred_element_type=jnp.float32)


def _dot_nt(a, b):
    return lax.dot_general(a, b, (((1,), (1,)), ((), ())), preferred_element_type=jnp.float32)


def _dot_tn(a, b):
    return lax.dot_general(a, b, (((0,), (0,)), ((), ())), preferred_element_type=jnp.float32)


def _layer_kernel(layer, tile, is_prefix, *refs):
    (x_ref, cos_ref, sin_ref, gpre_ref, gpost_ref, win_ref, wbr_ref, wout_ref,
     convw_ref, sink_ref, decay_ref, xi_ref, zeta_ref) = refs[:13]
    refs = refs[13:]
    if is_prefix:
        bucket_ref, relb_ref = refs[:2]
        out_ref, bias_ref, kst_ref, vst_ref, sst_ref, ust_ref = refs[2:8]
        refs = refs[8:]
    else:
        bias_ref, k0_ref, v0_ref, s0_ref, u0_ref = refs[:5]
        out_ref = refs[5]
        refs = refs[6:]
    (hb_s, q_s, kbuf, vbuf, ya_s, rq_s, qx_s, rk_s, kz_s, rv_s, yr_s, ubuf, state_s, y_s, mixed_s) = refs

    bf16 = jnp.bfloat16
    f32 = jnp.float32
    nb = tile // BLOCK
    t = pl.program_id(1)

    if is_prefix:
        kbuf[0:BLOCK, :] = jnp.zeros((BLOCK, BLOCK), bf16)
        vbuf[0:BLOCK, :] = jnp.zeros((BLOCK, BLOCK), bf16)
        state_s[...] = jnp.zeros_like(state_s)
        ubuf[0:CONV_HALO, :] = jnp.zeros((CONV_HALO, WIDTH), f32)
        bkt = bucket_ref[...]
        for hq in range(ATT_HEADS):
            acc = jnp.zeros((BLOCK, BLOCK), f32)
            for b in range(N_BUCKETS):
                acc = jnp.where(bkt == b, relb_ref[b, hq], acc)
            bias_ref[hq // ATT_GROUP, (hq % ATT_GROUP) * BLOCK:(hq % ATT_GROUP + 1) * BLOCK, :] = acc
        row = lax.broadcasted_iota(jnp.int32, (tile, 1), 0)
        valid = (row >= PAD_FRONT).astype(f32)
    else:
        @pl.when(t == 0)
        def _():
            kbuf[0:BLOCK, :] = k0_ref[...]
            vbuf[0:BLOCK, :] = v0_ref[...]
            state_s[...] = s0_ref[...]
            ubuf[0:CONV_HALO, :] = u0_ref[...]

    qrow = lax.broadcasted_iota(jnp.int32, (ATT_GROUP * BLOCK, BLOCK), 0) & (BLOCK - 1)
    kcol = lax.broadcasted_iota(jnp.int32, (ATT_GROUP * BLOCK, BLOCK), 1)
    from_cur = kcol <= qrow
    if is_prefix:
        first_mask = jnp.where(from_cur & (kcol >= PAD_FRONT), 0.0, NEG_INF).astype(f32)
    else:
        first_mask = jnp.where((t == 0) & jnp.logical_not(from_cur) & (kcol < PAD_FRONT), NEG_INF, 0.0).astype(f32)

    x = x_ref[...]
    ms = jnp.mean(x * x, axis=-1, keepdims=True)
    hb_s[...] = (x * lax.rsqrt(ms + RMS_EPS) * gpre_ref[...]).astype(bf16)

    def proj(a, b):
        return _dot(hb_s[...], win_ref[:, a:b])

    q_s[...] = (proj(O_AQ, O_AK) * (ATT_HEAD_DIM ** -0.5)).astype(bf16)
    kv = proj(O_AK, O_AG)
    kbuf[BLOCK:BLOCK + tile, :] = kv[:, :BLOCK].astype(bf16)
    vbuf[BLOCK:BLOCK + tile, :] = kv[:, BLOCK:].astype(bf16)

    def att_scores(j, kvh):
        rows = slice(j * BLOCK, (j + 1) * BLOCK)
        band = slice(j * BLOCK, (j + 2) * BLOCK)
        qs = jnp.concatenate(
            [q_s[rows, (kvh * ATT_GROUP + g) * ATT_HEAD_DIM:(kvh * ATT_GROUP + g + 1) * ATT_HEAD_DIM]
             for g in range(ATT_GROUP)], axis=0)
        return _dot_nt(qs, kbuf[band, kvh * ATT_HEAD_DIM:(kvh + 1) * ATT_HEAD_DIM])

    def att_softmax(j, kvh, s2):
        s = jnp.where(from_cur, s2[:, BLOCK:], s2[:, :BLOCK]) + bias_ref[kvh]
        if j == 0:
            s = s + first_mask
        ps = []
        for g in range(ATT_GROUP):
            sg = s[g * BLOCK:(g + 1) * BLOCK]
            sink = sink_ref[layer, kvh * ATT_GROUP + g]
            m = jnp.maximum(jnp.max(sg, axis=-1, keepdims=True), sink)
            p = jnp.exp(sg - m)
            denom = jnp.sum(p, axis=-1, keepdims=True) + jnp.exp(sink - m)
            ps.append(p * (1.0 / denom))
        pn = jnp.concatenate(ps, axis=0)
        pband = jnp.concatenate([jnp.where(from_cur, 0.0, pn), jnp.where(from_cur, pn, 0.0)], axis=1)
        return pband.astype(bf16)

    def att_values(j, kvh, pband):
        rows = slice(j * BLOCK, (j + 1) * BLOCK)
        band = slice(j * BLOCK, (j + 2) * BLOCK)
        o = _dot(pband, vbuf[band, kvh * ATT_HEAD_DIM:(kvh + 1) * ATT_HEAD_DIM])
        for g in range(ATT_GROUP):
            h0 = (kvh * ATT_GROUP + g) * ATT_HEAD_DIM
            ya_s[rows, h0:h0 + ATT_HEAD_DIM] = o[g * BLOCK:(g + 1) * BLOCK]

    cos_t = cos_ref[...]
    sin_t = sin_ref[...]

    def rotate(v):
        return v * cos_t + pltpu.roll(v, RET_HEAD_DIM // 2, 1) * sin_t

    def ret_q(pair):
        c0 = pair * 2 * RET_HEAD_DIM
        rq = proj(O_RQ + c0, O_RQ + c0 + 2 * RET_HEAD_DIM)
        for i in range(2):
            hs = slice(c0 + i * RET_HEAD_DIM, c0 + (i + 1) * RET_HEAD_DIM)
            qr = rotate(rq[:, i * RET_HEAD_DIM:(i + 1) * RET_HEAD_DIM])
            rq_s[:, hs] = qr.astype(bf16)
            qx_s[:, hs] = (qr * xi_ref[:, hs]).astype(bf16)

    def ret_k(pair):
        c0 = pair * 2 * RET_HEAD_DIM
        rk = proj(O_RK + c0, O_RK + c0 + 2 * RET_HEAD_DIM)
        for i in range(2):
            hs = slice(c0 + i * RET_HEAD_DIM, c0 + (i + 1) * RET_HEAD_DIM)
            kr = rotate(rk[:, i * RET_HEAD_DIM:(i + 1) * RET_HEAD_DIM]) * (RET_HEAD_DIM ** -0.5)
            if is_prefix:
                kr = kr * valid
            rk_s[:, hs] = kr.astype(bf16)
            kz_s[:, hs] = (kr * zeta_ref[:, hs]).astype(bf16)

    def ret_v(pair):
        c0 = pair * 2 * RET_HEAD_DIM
        rv_s[:, c0:c0 + 2 * RET_HEAD_DIM] = proj(O_RV + c0, O_RV + c0 + 2 * RET_HEAD_DIM).astype(bf16)

    units = [(j, kvh) for j in range(nb) for kvh in range(ATT_KV_HEADS)]
    fillers = [functools.partial(f, pair) for f in (ret_q, ret_k, ret_v) for pair in range(2)]
    scores_q = [att_scores(*units[0])]
    probs_q = []
    for i, unit in enumerate(units):
        if i + 1 < len(units):
            scores_q.append(att_scores(*units[i + 1]))
        if fillers:
            fillers.pop(0)()
        probs_q.append(att_softmax(*unit, scores_q[i]))
        if i >= 1:
            att_values(*units[i - 1], probs_q[i - 1])
            if fillers and i % 2 == 1:
                fillers.pop(0)()
    att_values(*units[-1], probs_q[-1])
    for f in fillers:
        f()

    y_s[0] = (ya_s[...] * _silu(proj(O_AG, O_RQ))).astype(bf16)

    heads = [slice(h * RET_HEAD_DIM, (h + 1) * RET_HEAD_DIM) for h in range(RET_HEADS)]
    blocks = [slice(j * BLOCK, (j + 1) * BLOCK) for j in range(nb)]
    scores = [[_dot_nt(rq_s[rows, hs], rk_s[rows, hs]) for rows in blocks] for hs in heads]
    chunk_kv = [[_dot_tn(kz_s[rows, hs], rv_s[rows, hs]) for rows in blocks] for hs in heads]
    outs = []
    for h, hs in enumerate(heads):
        st = state_s[h]
        for j, rows in enumerate(blocks):
            lhs = jnp.concatenate([(scores[h][j] * decay_ref[h]).astype(bf16), qx_s[rows, hs]], axis=1)
            rhs = jnp.concatenate([rv_s[rows, hs], st.astype(bf16)], axis=0)
            outs.append((rows, hs, _dot(lhs, rhs)))
            st = _GAMMA_CHUNK[h] * st + chunk_kv[h][j]
        state_s[h] = st
    for rows, hs, o in outs:
        mu = jnp.mean(o, axis=-1, keepdims=True)
        d = o - mu
        var = jnp.mean(d * d, axis=-1, keepdims=True)
        yr_s[rows, hs] = d * lax.rsqrt(var + GN_EPS)
    y_s[1] = (yr_s[...] * _silu(proj(O_RG, O_CB))).astype(bf16)

    u = proj(O_CC, O_CX) * proj(O_CX, O_CG)
    if is_prefix:
        u = u * valid
    ubuf[CONV_HALO:CONV_HALO + tile, :] = u
    conv = (convw_ref[0:1, :] * ubuf[CONV_HALO - 2:CONV_HALO - 2 + tile, :]
            + convw_ref[1:2, :] * ubuf[CONV_HALO - 1:CONV_HALO - 1 + tile, :]
            + convw_ref[2:3, :] * u)
    y_s[2] = (proj(O_CB, O_CC) * conv * _silu(proj(O_CG, O_MERGE))).astype(bf16)

    for n0 in range(0, D_MODEL, MERGE_CHUNK):
        acc = None
        for g in range(N_BRANCH):
            c0 = O_MERGE + g * D_MODEL + n0
            gate = _sigmoid(proj(c0, c0 + MERGE_CHUNK))
            term = gate * _dot(y_s[g], wbr_ref[g, :, n0:n0 + MERGE_CHUNK])
            acc = term if acc is None else acc + term
        mixed_s[:, n0:n0 + MERGE_CHUNK] = acc.astype(bf16)

    o = _dot(mixed_s[...], wout_ref[...])
    oms = jnp.mean(o * o, axis=-1, keepdims=True)
    out_ref[...] = x_ref[...] + o * lax.rsqrt(oms + RMS_EPS) * gpost_ref[...]

    ubuf[0:CONV_HALO, :] = ubuf[tile:tile + CONV_HALO, :]
    kbuf[0:BLOCK, :] = kbuf[tile:tile + BLOCK, :]
    vbuf[0:BLOCK, :] = vbuf[tile:tile + BLOCK, :]
    if is_prefix:
        kst_ref[...] = kbuf[0:BLOCK, :]
        vst_ref[...] = vbuf[0:BLOCK, :]
        sst_ref[...] = state_s[...]
        ust_ref[...] = ubuf[0:CONV_HALO, :]


def _const_spec(shape, index=None):
    index = (0,) * len(shape) if index is None else index
    return pl.BlockSpec(shape, lambda b, t: index, pipeline_mode=pl.Buffered(1))


def _layer_call(layer, x, cos_tab, sin_tab, params, tables, extra, *, tile, is_prefix):
    B, L, _ = x.shape
    assert L % tile == 0 and tile % BLOCK == 0
    gpre, gpost, win, wbr, wout, convw, sinks = params
    decay, xi_tab, zeta_tab = tables
    nb = tile // BLOCK
    xi_t = jnp.tile(xi_tab, (nb, 1))
    zeta_t = jnp.tile(zeta_tab, (nb, 1))
    bf16, f32 = jnp.bfloat16, jnp.float32
    bias_shape = (ATT_KV_HEADS, ATT_GROUP * BLOCK, BLOCK)

    in_specs = [
        pl.BlockSpec((None, tile, D_MODEL), lambda b, t: (b, t, 0)),
        pl.BlockSpec((tile, RET_HEAD_DIM), lambda b, t: (t, 0)),
        pl.BlockSpec((tile, RET_HEAD_DIM), lambda b, t: (t, 0)),
        _const_spec((None, 1, D_MODEL), (layer, 0, 0)),
        _const_spec((None, 1, D_MODEL), (layer, 0, 0)),
        _const_spec((None, D_MODEL, PROJ_WIDTH), (layer, 0, 0)),
        _const_spec((None, N_BRANCH, WIDTH, D_MODEL), (layer, 0, 0, 0)),
        _const_spec((None, D_MODEL, D_MODEL), (layer, 0, 0)),
        _const_spec((None, CONV_K, WIDTH), (layer, 0, 0)),
        pl.BlockSpec(memory_space=pltpu.SMEM),
        _const_spec((RET_HEADS, BLOCK, BLOCK)),
        _const_spec((tile, WIDTH)), _const_spec((tile, WIDTH)),
    ]
    out_shape = [jax.ShapeDtypeStruct((B, L, D_MODEL), f32)]
    out_specs = [pl.BlockSpec((None, tile, D_MODEL), lambda b, t: (b, t, 0))]
    state_shapes = [((BLOCK, BLOCK), bf16), ((BLOCK, BLOCK), bf16),
                    ((RET_HEADS, BLOCK, BLOCK), f32), ((CONV_HALO, WIDTH), f32)]
    if is_prefix:
        in_specs += [_const_spec((BLOCK, BLOCK)), pl.BlockSpec(memory_space=pltpu.SMEM)]
        for shape, dt in [(bias_shape, f32)] + state_shapes:
            out_shape.append(jax.ShapeDtypeStruct(shape, dt))
            out_specs.append(pl.BlockSpec(shape, lambda b, t, n=len(shape): (0,) * n))
    else:
        in_specs += [_const_spec(bias_shape), _const_spec((BLOCK, BLOCK)), _const_spec((BLOCK, BLOCK)),
                     _const_spec((RET_HEADS, BLOCK, BLOCK)),
                     _const_spec((CONV_HALO, WIDTH))]
    scratch = [
        pltpu.VMEM((tile, D_MODEL), bf16),
        pltpu.VMEM((tile, WIDTH), bf16),
        pltpu.VMEM((tile + BLOCK, BLOCK), bf16),
        pltpu.VMEM((tile + BLOCK, BLOCK), bf16),
        pltpu.VMEM((tile, WIDTH), f32),
        pltpu.VMEM((tile, WIDTH), bf16),
        pltpu.VMEM((tile, WIDTH), bf16),
        pltpu.VMEM((tile, WIDTH), bf16),
        pltpu.VMEM((tile, WIDTH), bf16),
        pltpu.VMEM((tile, WIDTH), bf16),
        pltpu.VMEM((tile, WIDTH), f32),
        pltpu.VMEM((tile + CONV_HALO, WIDTH), f32),
        pltpu.VMEM((RET_HEADS, BLOCK, BLOCK), f32),
        pltpu.VMEM((N_BRANCH, tile, WIDTH), bf16),
        pltpu.VMEM((tile, D_MODEL), bf16),
    ]
    return pl.pallas_call(
        functools.partial(_layer_kernel, layer, tile, is_prefix),
        grid=(B, L // tile),
        in_specs=in_specs,
        out_specs=out_specs,
        out_shape=out_shape,
        scratch_shapes=scratch,
        compiler_params=pltpu.CompilerParams(
            dimension_semantics=("arbitrary", "arbitrary"),
            vmem_limit_bytes=VMEM_LIMIT_BYTES),
        name="hybrid_layer_prefix" if is_prefix else "hybrid_layer_main",
    )(x, cos_tab, sin_tab, gpre, gpost, win, wbr, wout, convw, sinks, decay, xi_t, zeta_t, *extra)


def _t5_bucket_table():
    r = np.arange(BLOCK)[:, None]
    c = np.arange(BLOCK)[None, :]
    n = np.where(c <= r, r - c, BLOCK + r - c)
    nf = np.maximum(n, 1).astype(np.float32)
    large = MAX_EXACT + (np.log(nf / np.float32(MAX_EXACT)) / np.float32(math.log(MAX_DISTANCE / MAX_EXACT))
                         * np.float32(N_BUCKETS - MAX_EXACT)).astype(np.int32)
    large = np.minimum(large, N_BUCKETS - 1)
    return np.where(n < MAX_EXACT, n, large).astype(np.int32)


def _position_tables(n_pos):
    half = RET_HEAD_DIM // 2
    theta = 1.0 / (ROT_BASE ** jnp.linspace(0.0, 1.0, half, dtype=jnp.float32))
    ang = jnp.arange(n_pos).astype(jnp.float32)[:, None] * theta[None, :]
    cos, sin = jnp.cos(ang), jnp.sin(ang)
    return jnp.concatenate([cos, cos], axis=1), jnp.concatenate([-sin, sin], axis=1)


def _retention_tables():
    log_gamma = jnp.asarray(_LOG_GAMMA)
    i = jnp.arange(BLOCK, dtype=jnp.float32)
    diff = i[:, None] - i[None, :]
    decay = jnp.where(diff[None] >= 0, jnp.exp(diff[None] * log_gamma[:, None, None]), 0.0)
    zeta = jnp.exp((BLOCK - 1 - i)[None, :] * log_gamma[:, None])
    xi = jnp.exp((i + 1)[None, :] * log_gamma[:, None])
    expand = lambda a: jnp.repeat(a.T, RET_HEAD_DIM, axis=1)
    return decay, expand(xi), expand(zeta)


def kernel(x, meta_tokens, rel_bias, norm_pre, w_in, conv_w, attn_sinks, w_branch, w_out, norm_post):
    B, S, _ = x.shape
    f32, bf16 = jnp.float32, jnp.bfloat16

    bucket = jnp.asarray(_t5_bucket_table())
    cos_tab, sin_tab = _position_tables(N_META + S)
    cos_pre = jnp.concatenate([jnp.ones((PAD_FRONT, RET_HEAD_DIM), f32), cos_tab[:N_META]], axis=0)
    sin_pre = jnp.concatenate([jnp.zeros((PAD_FRONT, RET_HEAD_DIM), f32), sin_tab[:N_META]], axis=0)
    tables = _retention_tables()
    params = (norm_pre[:, None, :], norm_post[:, None, :], w_in.astype(bf16), w_branch.astype(bf16),
              w_out.astype(bf16), conv_w, attn_sinks)

    h_pre = jnp.concatenate([jnp.zeros((PAD_FRONT, D_MODEL), f32), meta_tokens.astype(f32)], axis=0)[None]
    h = x
    for l in range(DEPTH):
        h_pre, bias_tab, k0, v0, s0, u0 = _layer_call(
            l, h_pre, cos_pre, sin_pre, params, tables, (bucket, rel_bias), tile=BLOCK, is_prefix=True)
        (h,) = _layer_call(
            l, h, cos_tab[N_META:], sin_tab[N_META:], params, tables,
            (bias_tab, k0, v0, s0, u0), tile=TILE, is_prefix=False)
    return h
```

```python
import functools
import math

import numpy as np
import jax
import jax.numpy as jnp
from jax import lax
from jax.experimental import pallas as pl
from jax.experimental.pallas import tpu as pltpu

D_MODEL = 1024
DEPTH = 2
N_META = 16
BLOCK = 128
PAD_FRONT = BLOCK - N_META

ATT_HEADS = 8
ATT_KV_HEADS = 2
ATT_GROUP = ATT_HEADS // ATT_KV_HEADS
ATT_HEAD_DIM = 64
N_BUCKETS = 32
MAX_EXACT = N_BUCKETS // 2
MAX_DISTANCE = 128

RET_HEADS = 4
RET_HEAD_DIM = 128
ROT_BASE = 10000.0

CONV_K = 3
WIDTH = 512
N_BRANCH = 3
RMS_EPS = 1e-6
GN_EPS = 1e-6
NEG_INF = -1e30

_SPLITS = (512, 128, 128, 512, 512, 512, 512, 512, 512, 512, 512, 512, N_BRANCH * D_MODEL)
_OFF = [0] + [int(v) for v in np.cumsum(_SPLITS)]
(O_AQ, O_AK, O_AV, O_AG, O_RQ, O_RK, O_RV, O_RG, O_CB, O_CC, O_CX, O_CG, O_MERGE, PROJ_WIDTH) = _OFF

TILE = 512
CONV_HALO = 8
ROW_GROUP = 256
MERGE_CHUNK = 512
VMEM_LIMIT_BYTES = 56 * 1024 * 1024

_LOG_GAMMA = np.log1p(-(np.float32(2.0) ** (-5.0 - np.arange(RET_HEADS, dtype=np.float32)))).astype(np.float32)
_GAMMA_CHUNK = [float(v) for v in np.exp(np.float32(BLOCK) * _LOG_GAMMA).astype(np.float32)]


def _sigmoid(v):
    return 1.0 / (1.0 + jnp.exp(-v))


def _silu(v):
    return v * _sigmoid(v)


def _dot(a, b):
    return jnp.dot(a, b, preferred_element_type=jnp.float32)


def _dot_nt(a, b):
    return lax.dot_general(a, b, (((1,), (1,)), ((), ())), preferred_element_type=jnp.float32)


def _dot_tn(a, b):
    return lax.dot_general(a, b, (((0,), (0,)), ((), ())), preferred_element_type=jnp.float32)


def _layer_kernel(layer, tile, is_prefix, *refs):
    (x_ref, cos_ref, sin_ref, gpre_ref, gpost_ref, win_ref, wbr_ref, wout_ref,
     convw_ref, sink_ref, decay_ref, xi_ref, zeta_ref) = refs[:13]
    refs = refs[13:]
    if is_prefix:
        bucket_ref, relb_ref = refs[:2]
        out_ref, bias_ref, kst_ref, vst_ref, sst_ref, ust_ref = refs[2:8]
        refs = refs[8:]
    else:
        bias_ref, k0_ref, v0_ref, s0_ref, u0_ref = refs[:5]
        out_ref = refs[5]
        refs = refs[6:]
    (hb_s, q_s, kbuf, vbuf, ya_s, rq_s, qx_s, rk_s, kz_s, rv_s, yr_s, ubuf, state_s, y_s, mixed_s) = refs

    bf16 = jnp.bfloat16
    f32 = jnp.float32
    nb = tile // BLOCK
    row_group = min(tile, ROW_GROUP)
    t = pl.program_id(1)

    if is_prefix:
        kbuf[0:BLOCK, :] = jnp.zeros((BLOCK, BLOCK), bf16)
        vbuf[0:BLOCK, :] = jnp.zeros((BLOCK, BLOCK), bf16)
        state_s[...] = jnp.zeros_like(state_s)
        ubuf[0:CONV_HALO, :] = jnp.zeros((CONV_HALO, WIDTH), f32)
        bkt = bucket_ref[...]
        for hq in range(ATT_HEADS):
            acc = jnp.zeros((BLOCK, BLOCK), f32)
            for b in range(N_BUCKETS):
                acc = jnp.where(bkt == b, relb_ref[b, hq], acc)
            bias_ref[hq // ATT_GROUP, (hq % ATT_GROUP) * BLOCK:(hq % ATT_GROUP + 1) * BLOCK, :] = acc
        row = lax.broadcasted_iota(jnp.int32, (tile, 1), 0)
        valid = (row >= PAD_FRONT).astype(f32)
    else:
        @pl.when(t == 0)
        def _():
            kbuf[0:BLOCK, :] = k0_ref[...]
            vbuf[0:BLOCK, :] = v0_ref[...]
            state_s[...] = s0_ref[...]
            ubuf[0:CONV_HALO, :] = u0_ref[...]

    qrow = lax.broadcasted_iota(jnp.int32, (ATT_GROUP * BLOCK, BLOCK), 0) & (BLOCK - 1)
    kcol = lax.broadcasted_iota(jnp.int32, (ATT_GROUP * BLOCK, BLOCK), 1)
    from_cur = kcol <= qrow
    if is_prefix:
        first_mask = jnp.where(from_cur & (kcol >= PAD_FRONT), 0.0, NEG_INF).astype(f32)
    else:
        first_mask = jnp.where((t == 0) & jnp.logical_not(from_cur) & (kcol < PAD_FRONT), NEG_INF, 0.0).astype(f32)

    for r0 in range(0, tile, row_group):
        rows = slice(r0, r0 + row_group)
        x = x_ref[rows, :]
        ms = jnp.mean(x * x, axis=-1, keepdims=True)
        hb = (x * lax.rsqrt(ms + RMS_EPS) * gpre_ref[...]).astype(bf16)
        hb_s[rows, :] = hb
        q_s[rows, :] = (_dot(hb, win_ref[:, O_AQ:O_AK]) * (ATT_HEAD_DIM ** -0.5)).astype(bf16)
        kv = _dot(hb, win_ref[:, O_AK:O_AG])
        kbuf[BLOCK + r0:BLOCK + r0 + row_group, :] = kv[:, :BLOCK].astype(bf16)
        vbuf[BLOCK + r0:BLOCK + r0 + row_group, :] = kv[:, BLOCK:].astype(bf16)

    def proj(a, b):
        return _dot(hb_s[...], win_ref[:, a:b])


    def att_scores(j, kvh):
        rows = slice(j * BLOCK, (j + 1) * BLOCK)
        band = slice(j * BLOCK, (j + 2) * BLOCK)
        qs = jnp.concatenate(
            [q_s[rows, (kvh * ATT_GROUP + g) * ATT_HEAD_DIM:(kvh * ATT_GROUP + g + 1) * ATT_HEAD_DIM]
             for g in range(ATT_GROUP)], axis=0)
        return _dot_nt(qs, kbuf[band, kvh * ATT_HEAD_DIM:(kvh + 1) * ATT_HEAD_DIM])

    def att_softmax(j, kvh, s2):
        s = jnp.where(from_cur, s2[:, BLOCK:], s2[:, :BLOCK]) + bias_ref[kvh]
        if j == 0:
            s = s + first_mask
        ps = []
        for g in range(ATT_GROUP):
            sg = s[g * BLOCK:(g + 1) * BLOCK]
            sink = sink_ref[layer, kvh * ATT_GROUP + g]
            m = jnp.maximum(jnp.max(sg, axis=-1, keepdims=True), sink)
            p = jnp.exp(sg - m)
            denom = jnp.sum(p, axis=-1, keepdims=True) + jnp.exp(sink - m)
            ps.append(p * (1.0 / denom))
        pn = jnp.concatenate(ps, axis=0)
        pband = jnp.concatenate([jnp.where(from_cur, 0.0, pn), jnp.where(from_cur, pn, 0.0)], axis=1)
        return pband.astype(bf16)

    def att_values(j, kvh, pband):
        rows = slice(j * BLOCK, (j + 1) * BLOCK)
        band = slice(j * BLOCK, (j + 2) * BLOCK)
        o = _dot(pband, vbuf[band, kvh * ATT_HEAD_DIM:(kvh + 1) * ATT_HEAD_DIM])
        for g in range(ATT_GROUP):
            h0 = (kvh * ATT_GROUP + g) * ATT_HEAD_DIM
            ya_s[rows, h0:h0 + ATT_HEAD_DIM] = o[g * BLOCK:(g + 1) * BLOCK]

    cos_t = cos_ref[...]
    sin_t = sin_ref[...]

    def rotate(v):
        return v * cos_t + pltpu.roll(v, RET_HEAD_DIM // 2, 1) * sin_t

    def ret_q(pair):
        c0 = pair * 2 * RET_HEAD_DIM
        rq = proj(O_RQ + c0, O_RQ + c0 + 2 * RET_HEAD_DIM)
        for i in range(2):
            hs = slice(c0 + i * RET_HEAD_DIM, c0 + (i + 1) * RET_HEAD_DIM)
            qr = rotate(rq[:, i * RET_HEAD_DIM:(i + 1) * RET_HEAD_DIM])
            rq_s[:, hs] = qr.astype(bf16)
            qx_s[:, hs] = (qr * xi_ref[:, hs]).astype(bf16)

    def ret_k(pair):
        c0 = pair * 2 * RET_HEAD_DIM
        rk = proj(O_RK + c0, O_RK + c0 + 2 * RET_HEAD_DIM)
        for i in range(2):
            hs = slice(c0 + i * RET_HEAD_DIM, c0 + (i + 1) * RET_HEAD_DIM)
            kr = rotate(rk[:, i * RET_HEAD_DIM:(i + 1) * RET_HEAD_DIM]) * (RET_HEAD_DIM ** -0.5)
            if is_prefix:
                kr = kr * valid
            rk_s[:, hs] = kr.astype(bf16)
            kz_s[:, hs] = (kr * zeta_ref[:, hs]).astype(bf16)

    def ret_v(pair):
        c0 = pair * 2 * RET_HEAD_DIM
        rv_s[:, c0:c0 + 2 * RET_HEAD_DIM] = proj(O_RV + c0, O_RV + c0 + 2 * RET_HEAD_DIM).astype(bf16)

    units = [(j, kvh) for j in range(nb) for kvh in range(ATT_KV_HEADS)]
    fillers = [functools.partial(f, pair) for f in (ret_q, ret_k, ret_v) for pair in range(2)]
    scores_q = [att_scores(*units[0])]
    probs_q = []
    for i, unit in enumerate(units):
        if i + 1 < len(units):
            scores_q.append(att_scores(*units[i + 1]))
        if fillers:
            fillers.pop(0)()
        probs_q.append(att_softmax(*unit, scores_q[i]))
        if i >= 1:
            att_values(*units[i - 1], probs_q[i - 1])
            if fillers and i % 2 == 1:
                fillers.pop(0)()
    att_values(*units[-1], probs_q[-1])
    for f in fillers:
        f()

    y_s[0] = (ya_s[...] * _silu(proj(O_AG, O_RQ))).astype(bf16)

    heads = [slice(h * RET_HEAD_DIM, (h + 1) * RET_HEAD_DIM) for h in range(RET_HEADS)]
    blocks = [slice(j * BLOCK, (j + 1) * BLOCK) for j in range(nb)]
    half_w = WIDTH // 2

    def conv_u(half):
        c0 = half * half_w
        u = proj(O_CC + c0, O_CC + c0 + half_w) * proj(O_CX + c0, O_CX + c0 + half_w)
        if is_prefix:
            u = u * valid
        ubuf[CONV_HALO:CONV_HALO + tile, c0:c0 + half_w] = u

    def conv_y(half):
        cs = slice(half * half_w, (half + 1) * half_w)
        conv = (convw_ref[0:1, cs] * ubuf[CONV_HALO - 2:CONV_HALO - 2 + tile, cs]
                + convw_ref[1:2, cs] * ubuf[CONV_HALO - 1:CONV_HALO - 1 + tile, cs]
                + convw_ref[2:3, cs] * ubuf[CONV_HALO:CONV_HALO + tile, cs])
        y_s[2, :, cs] = (proj(O_CB + cs.start, O_CB + cs.stop) * conv
                         * _silu(proj(O_CG + cs.start, O_CG + cs.stop))).astype(bf16)

    def ret_dots(h):
        hs = heads[h]
        return ([_dot_nt(rq_s[rows, hs], rk_s[rows, hs]) for rows in blocks],
                [_dot_tn(kz_s[rows, hs], rv_s[rows, hs]) for rows in blocks])

    def ret_mix(h, scores, chunk_kv):
        hs = heads[h]
        st = state_s[h]
        outs = []
        for j, rows in enumerate(blocks):
            lhs = jnp.concatenate([(scores[j] * decay_ref[h]).astype(bf16), qx_s[rows, hs]], axis=1)
            rhs = jnp.concatenate([rv_s[rows, hs], st.astype(bf16)], axis=0)
            outs.append(_dot(lhs, rhs))
            st = _GAMMA_CHUNK[h] * st + chunk_kv[j]
        state_s[h] = st
        return outs

    def ret_norm(h, outs):
        for rows, o in zip(blocks, outs):
            mu = jnp.mean(o, axis=-1, keepdims=True)
            d = o - mu
            var = jnp.mean(d * d, axis=-1, keepdims=True)
            yr_s[rows, heads[h]] = d * lax.rsqrt(var + GN_EPS)

    fillers = [functools.partial(f, half) for f in (conv_u, conv_y) for half in range(2)]
    dots = ret_dots(0)
    mixed = None
    for h in range(RET_HEADS):
        nxt = ret_dots(h + 1) if h + 1 < RET_HEADS else None
        fillers.pop(0)()
        prev, mixed = mixed, ret_mix(h, *dots)
        if h >= 1:
            ret_norm(h - 1, prev)
        dots = nxt
    ret_norm(RET_HEADS - 1, mixed)
    y_s[1] = (yr_s[...] * _silu(proj(O_RG, O_CB))).astype(bf16)

    for n0 in range(0, D_MODEL, MERGE_CHUNK):
        acc = None
        for g in range(N_BRANCH):
            c0 = O_MERGE + g * D_MODEL + n0
            gate = _sigmoid(proj(c0, c0 + MERGE_CHUNK))
            term = gate * _dot(y_s[g], wbr_ref[g, :, n0:n0 + MERGE_CHUNK])
            acc = term if acc is None else acc + term
        mixed_s[:, n0:n0 + MERGE_CHUNK] = acc.astype(bf16)

    for r0 in range(0, tile, row_group):
        rows = slice(r0, r0 + row_group)
        o = _dot(mixed_s[rows, :], wout_ref[...])
        oms = jnp.mean(o * o, axis=-1, keepdims=True)
        out_ref[rows, :] = x_ref[rows, :] + o * lax.rsqrt(oms + RMS_EPS) * gpost_ref[...]

    ubuf[0:CONV_HALO, :] = ubuf[tile:tile + CONV_HALO, :]
    kbuf[0:BLOCK, :] = kbuf[tile:tile + BLOCK, :]
    vbuf[0:BLOCK, :] = vbuf[tile:tile + BLOCK, :]
    if is_prefix:
        kst_ref[...] = kbuf[0:BLOCK, :]
        vst_ref[...] = vbuf[0:BLOCK, :]
        sst_ref[...] = state_s[...]
        ust_ref[...] = ubuf[0:CONV_HALO, :]


def _const_spec(shape, index=None):
    index = (0,) * len(shape) if index is None else index
    return pl.BlockSpec(shape, lambda b, t: index, pipeline_mode=pl.Buffered(1))


def _layer_call(layer, x, cos_tab, sin_tab, params, tables, extra, *, tile, is_prefix):
    B, L, _ = x.shape
    assert L % tile == 0 and tile % BLOCK == 0
    gpre, gpost, win, wbr, wout, convw, sinks = params
    decay, xi_tab, zeta_tab = tables
    nb = tile // BLOCK
    xi_t = jnp.tile(xi_tab, (nb, 1))
    zeta_t = jnp.tile(zeta_tab, (nb, 1))
    bf16, f32 = jnp.bfloat16, jnp.float32
    bias_shape = (ATT_KV_HEADS, ATT_GROUP * BLOCK, BLOCK)

    in_specs = [
        pl.BlockSpec((None, tile, D_MODEL), lambda b, t: (b, t, 0)),
        pl.BlockSpec((tile, RET_HEAD_DIM), lambda b, t: (t, 0)),
        pl.BlockSpec((tile, RET_HEAD_DIM), lambda b, t: (t, 0)),
        _const_spec((None, 1, D_MODEL), (layer, 0, 0)),
        _const_spec((None, 1, D_MODEL), (layer, 0, 0)),
        _const_spec((None, D_MODEL, PROJ_WIDTH), (layer, 0, 0)),
        _const_spec((None, N_BRANCH, WIDTH, D_MODEL), (layer, 0, 0, 0)),
        _const_spec((None, D_MODEL, D_MODEL), (layer, 0, 0)),
        _const_spec((None, CONV_K, WIDTH), (layer, 0, 0)),
        pl.BlockSpec(memory_space=pltpu.SMEM),
        _const_spec((RET_HEADS, BLOCK, BLOCK)),
        _const_spec((tile, WIDTH)), _const_spec((tile, WIDTH)),
    ]
    out_shape = [jax.ShapeDtypeStruct((B, L, D_MODEL), f32)]
    out_specs = [pl.BlockSpec((None, tile, D_MODEL), lambda b, t: (b, t, 0))]
    state_shapes = [((BLOCK, BLOCK), bf16), ((BLOCK, BLOCK), bf16),
                    ((RET_HEADS, BLOCK, BLOCK), f32), ((CONV_HALO, WIDTH), f32)]
    if is_prefix:
        in_specs += [_const_spec((BLOCK, BLOCK)), pl.BlockSpec(memory_space=pltpu.SMEM)]
        for shape, dt in [(bias_shape, f32)] + state_shapes:
            out_shape.append(jax.ShapeDtypeStruct(shape, dt))
            out_specs.append(pl.BlockSpec(shape, lambda b, t, n=len(shape): (0,) * n))
    else:
        in_specs += [_const_spec(bias_shape), _const_spec((BLOCK, BLOCK)), _const_spec((BLOCK, BLOCK)),
                     _const_spec((RET_HEADS, BLOCK, BLOCK)),
                     _const_spec((CONV_HALO, WIDTH))]
    scratch = [
        pltpu.VMEM((tile, D_MODEL), bf16),
        pltpu.VMEM((tile, WIDTH), bf16),
        pltpu.VMEM((tile + BLOCK, BLOCK), bf16),
        pltpu.VMEM((tile + BLOCK, BLOCK), bf16),
        pltpu.VMEM((tile, WIDTH), f32),
        pltpu.VMEM((tile, WIDTH), bf16),
        pltpu.VMEM((tile, WIDTH), bf16),
        pltpu.VMEM((tile, WIDTH), bf16),
        pltpu.VMEM((tile, WIDTH), bf16),
        pltpu.VMEM((tile, WIDTH), bf16),
        pltpu.VMEM((tile, WIDTH), f32),
        pltpu.VMEM((tile + CONV_HALO, WIDTH), f32),
        pltpu.VMEM((RET_HEADS, BLOCK, BLOCK), f32),
        pltpu.VMEM((N_BRANCH, tile, WIDTH), bf16),
        pltpu.VMEM((tile, D_MODEL), bf16),
    ]
    return pl.pallas_call(
        functools.partial(_layer_kernel, layer, tile, is_prefix),
        grid=(B, L // tile),
        in_specs=in_specs,
        out_specs=out_specs,
        out_shape=out_shape,
        scratch_shapes=scratch,
        compiler_params=pltpu.CompilerParams(
            dimension_semantics=("arbitrary", "arbitrary"),
            vmem_limit_bytes=VMEM_LIMIT_BYTES),
        name="hybrid_layer_prefix" if is_prefix else "hybrid_layer_main",
    )(x, cos_tab, sin_tab, gpre, gpost, win, wbr, wout, convw, sinks, decay, xi_t, zeta_t, *extra)


def _t5_bucket_table():
    r = np.arange(BLOCK)[:, None]
    c = np.arange(BLOCK)[None, :]
    n = np.where(c <= r, r - c, BLOCK + r - c)
    nf = np.maximum(n, 1).astype(np.float32)
    large = MAX_EXACT + (np.log(nf / np.float32(MAX_EXACT)) / np.float32(math.log(MAX_DISTANCE / MAX_EXACT))
                         * np.float32(N_BUCKETS - MAX_EXACT)).astype(np.int32)
    large = np.minimum(large, N_BUCKETS - 1)
    return np.where(n < MAX_EXACT, n, large).astype(np.int32)


def _position_tables(n_pos):
    half = RET_HEAD_DIM // 2
    theta = 1.0 / (ROT_BASE ** jnp.linspace(0.0, 1.0, half, dtype=jnp.float32))
    ang = jnp.arange(n_pos).astype(jnp.float32)[:, None] * theta[None, :]
    cos, sin = jnp.cos(ang), jnp.sin(ang)
    return jnp.concatenate([cos, cos], axis=1), jnp.concatenate([-sin, sin], axis=1)


def _retention_tables():
    log_gamma = jnp.asarray(_LOG_GAMMA)
    i = jnp.arange(BLOCK, dtype=jnp.float32)
    diff = i[:, None] - i[None, :]
    decay = jnp.where(diff[None] >= 0, jnp.exp(diff[None] * log_gamma[:, None, None]), 0.0)
    zeta = jnp.exp((BLOCK - 1 - i)[None, :] * log_gamma[:, None])
    xi = jnp.exp((i + 1)[None, :] * log_gamma[:, None])
    expand = lambda a: jnp.repeat(a.T, RET_HEAD_DIM, axis=1)
    return decay, expand(xi), expand(zeta)


def kernel(x, meta_tokens, rel_bias, norm_pre, w_in, conv_w, attn_sinks, w_branch, w_out, norm_post):
    B, S, _ = x.shape
    f32, bf16 = jnp.float32, jnp.bfloat16

    bucket = jnp.asarray(_t5_bucket_table())
    cos_tab, sin_tab = _position_tables(N_META + S)
    cos_pre = jnp.concatenate([jnp.ones((PAD_FRONT, RET_HEAD_DIM), f32), cos_tab[:N_META]], axis=0)
    sin_pre = jnp.concatenate([jnp.zeros((PAD_FRONT, RET_HEAD_DIM), f32), sin_tab[:N_META]], axis=0)
    tables = _retention_tables()
    params = (norm_pre[:, None, :], norm_post[:, None, :], w_in.astype(bf16), w_branch.astype(bf16),
              w_out.astype(bf16), conv_w, attn_sinks)

    h_pre = jnp.concatenate([jnp.zeros((PAD_FRONT, D_MODEL), f32), meta_tokens.astype(f32)], axis=0)[None]
    h = x
    for l in range(DEPTH):
        h_pre, bias_tab, k0, v0, s0, u0 = _layer_call(
            l, h_pre, cos_pre, sin_pre, params, tables, (bucket, rel_bias), tile=BLOCK, is_prefix=True)
        (h,) = _layer_call(
            l, h, cos_tab[N_META:], sin_tab[N_META:], params, tables,
            (bias_tab, k0, v0, s0, u0), tile=TILE, is_prefix=False)
    return h
```

```python
import functools
import math

import numpy as np
import jax
import jax.numpy as jnp
from jax import lax
from jax.experimental import pallas as pl
from jax.experimental.pallas import tpu as pltpu

D_MODEL = 1024
DEPTH = 2
N_META = 16
BLOCK = 128
PAD_FRONT = BLOCK - N_META

ATT_HEADS = 8
ATT_KV_HEADS = 2
ATT_GROUP = ATT_HEADS // ATT_KV_HEADS
ATT_HEAD_DIM = 64
N_BUCKETS = 32
MAX_EXACT = N_BUCKETS // 2
MAX_DISTANCE = 128

RET_HEADS = 4
RET_HEAD_DIM = 128
ROT_BASE = 10000.0

CONV_K = 3
WIDTH = 512
N_BRANCH = 3
RMS_EPS = 1e-6
GN_EPS = 1e-6
NEG_INF = -1e30

_SPLITS = (512, 128, 128, 512, 512, 512, 512, 512, 512, 512, 512, 512, N_BRANCH * D_MODEL)
_OFF = [0] + [int(v) for v in np.cumsum(_SPLITS)]
(O_AQ, O_AK, O_AV, O_AG, O_RQ, O_RK, O_RV, O_RG, O_CB, O_CC, O_CX, O_CG, O_MERGE, PROJ_WIDTH) = _OFF

TILE = 512
CONV_HALO = 8
ROW_GROUP = 256
MERGE_CHUNK = 512
VMEM_LIMIT_BYTES = 56 * 1024 * 1024

_LOG_GAMMA = np.log1p(-(np.float32(2.0) ** (-5.0 - np.arange(RET_HEADS, dtype=np.float32)))).astype(np.float32)
_GAMMA_CHUNK = [float(v) for v in np.exp(np.float32(BLOCK) * _LOG_GAMMA).astype(np.float32)]


def _sigmoid(v):
    return 1.0 / (1.0 + jnp.exp(-v))


def _silu(v):
    return v * _sigmoid(v)


def _dot(a, b):
    return jnp.dot(a, b, preferred_element_type=jnp.float32)


def _dot_nt(a, b):
    return lax.dot_general(a, b, (((1,), (1,)), ((), ())), preferred_element_type=jnp.float32)


def _dot_tn(a, b):
    return lax.dot_general(a, b, (((0,), (0,)), ((), ())), preferred_element_type=jnp.float32)


def _layer_kernel(layer, tile, is_prefix, *refs):
    (x_ref, cos_ref, sin_ref, gpre_ref, gpost_ref, win_ref, wbr_ref, wout_ref,
     convw_ref, sink_ref, decay_ref, xi_ref, zeta_ref) = refs[:13]
    refs = refs[13:]
    if is_prefix:
        bucket_ref, relb_ref = refs[:2]
        out_ref, bias_ref, kst_ref, vst_ref, sst_ref, ust_ref = refs[2:8]
        refs = refs[8:]
    else:
        bias_ref, k0_ref, v0_ref, s0_ref, u0_ref = refs[:5]
        out_ref = refs[5]
        refs = refs[6:]
    (hb_s, q_s, kbuf, vt_s, ya_s, rq_s, qx_s, rk_s, kz_s, rv_s, yr_s, ubuf, state_s, y_s, mixed_s) = refs

    bf16 = jnp.bfloat16
    f32 = jnp.float32
    nb = tile // BLOCK
    row_group = min(tile, ROW_GROUP)
    t = pl.program_id(1)

    if is_prefix:
        kbuf[0:BLOCK, :] = jnp.zeros((BLOCK, BLOCK), bf16)
        vt_s[:, 0:BLOCK] = jnp.zeros((BLOCK, BLOCK), bf16)
        state_s[...] = jnp.zeros_like(state_s)
        ubuf[0:CONV_HALO, :] = jnp.zeros((CONV_HALO, WIDTH), f32)
        bkt = bucket_ref[...]
        for hq in range(ATT_HEADS):
            acc = jnp.zeros((BLOCK, BLOCK), f32)
            for b in range(N_BUCKETS):
                acc = jnp.where(bkt == b, relb_ref[b, hq], acc)
            bias_ref[hq // ATT_GROUP, :, (hq % ATT_GROUP) * BLOCK:(hq % ATT_GROUP + 1) * BLOCK] = acc
        row = lax.broadcasted_iota(jnp.int32, (tile, 1), 0)
        valid = (row >= PAD_FRONT).astype(f32)
    else:
        @pl.when(t == 0)
        def _():
            kbuf[0:BLOCK, :] = k0_ref[...]
            vt_s[:, 0:BLOCK] = v0_ref[...]
            state_s[...] = s0_ref[...]
            ubuf[0:CONV_HALO, :] = u0_ref[...]

    kidx = lax.broadcasted_iota(jnp.int32, (BLOCK, ATT_GROUP * BLOCK), 0)
    qidx = lax.broadcasted_iota(jnp.int32, (BLOCK, ATT_GROUP * BLOCK), 1) & (BLOCK - 1)
    from_cur = kidx <= qidx
    if is_prefix:
        first_mask = jnp.where(from_cur & (kidx >= PAD_FRONT), 0.0, NEG_INF).astype(f32)
    else:
        first_mask = jnp.where((t == 0) & jnp.logical_not(from_cur) & (kidx < PAD_FRONT), NEG_INF, 0.0).astype(f32)

    for r0 in range(0, tile, row_group):
        rows = slice(r0, r0 + row_group)
        x = x_ref[rows, :]
        ms = jnp.mean(x * x, axis=-1, keepdims=True)
        hb = (x * lax.rsqrt(ms + RMS_EPS) * gpre_ref[...]).astype(bf16)
        hb_s[rows, :] = hb
        q_s[rows, :] = (_dot(hb, win_ref[:, O_AQ:O_AK]) * (ATT_HEAD_DIM ** -0.5)).astype(bf16)
        kv = _dot(hb, win_ref[:, O_AK:O_AG])
        kbuf[BLOCK + r0:BLOCK + r0 + row_group, :] = kv[:, :BLOCK].astype(bf16)
        vt_s[:, BLOCK + r0:BLOCK + r0 + row_group] = kv[:, BLOCK:].T.astype(bf16)

    def proj(a, b):
        return _dot(hb_s[...], win_ref[:, a:b])


    def att_scores(j, kvh):
        rows = slice(j * BLOCK, (j + 1) * BLOCK)
        band = slice(j * BLOCK, (j + 2) * BLOCK)
        qs = jnp.concatenate(
            [q_s[rows, (kvh * ATT_GROUP + g) * ATT_HEAD_DIM:(kvh * ATT_GROUP + g + 1) * ATT_HEAD_DIM]
             for g in range(ATT_GROUP)], axis=0)
        return _dot_nt(kbuf[band, kvh * ATT_HEAD_DIM:(kvh + 1) * ATT_HEAD_DIM], qs)

    def att_softmax(j, kvh, s2):
        s = jnp.where(from_cur, s2[BLOCK:], s2[:BLOCK]) + bias_ref[kvh]
        if j == 0:
            s = s + first_mask
        ps = []
        for g in range(ATT_GROUP):
            sg = s[:, g * BLOCK:(g + 1) * BLOCK]
            sink = sink_ref[layer, kvh * ATT_GROUP + g]
            m = jnp.maximum(jnp.max(sg, axis=0, keepdims=True), sink)
            p = jnp.exp(sg - m)
            denom = jnp.sum(p, axis=0, keepdims=True) + jnp.exp(sink - m)
            ps.append(p * (1.0 / denom))
        pn = jnp.concatenate(ps, axis=1)
        pband = jnp.concatenate([jnp.where(from_cur, 0.0, pn), jnp.where(from_cur, pn, 0.0)], axis=0)
        return pband.astype(bf16)

    def att_values(j, kvh, pband):
        rows = slice(j * BLOCK, (j + 1) * BLOCK)
        band = slice(j * BLOCK, (j + 2) * BLOCK)
        ot = _dot(vt_s[kvh * ATT_HEAD_DIM:(kvh + 1) * ATT_HEAD_DIM, band], pband)
        for pair in range(ATT_GROUP // 2):
            two = jnp.concatenate([ot[:, (2 * pair) * BLOCK:(2 * pair + 1) * BLOCK],
                                   ot[:, (2 * pair + 1) * BLOCK:(2 * pair + 2) * BLOCK]], axis=0)
            h0 = (kvh * ATT_GROUP + 2 * pair) * ATT_HEAD_DIM
            ya_s[rows, h0:h0 + 2 * ATT_HEAD_DIM] = two.T

    cos_t = cos_ref[...]
    sin_t = sin_ref[...]

    def rotate(v):
        return v * cos_t + pltpu.roll(v, RET_HEAD_DIM // 2, 1) * sin_t

    def ret_q(pair):
        c0 = pair * 2 * RET_HEAD_DIM
        rq = proj(O_RQ + c0, O_RQ + c0 + 2 * RET_HEAD_DIM)
        for i in range(2):
            hs = slice(c0 + i * RET_HEAD_DIM, c0 + (i + 1) * RET_HEAD_DIM)
            qr = rotate(rq[:, i * RET_HEAD_DIM:(i + 1) * RET_HEAD_DIM])
            rq_s[:, hs] = qr.astype(bf16)
            qx_s[:, hs] = (qr * xi_ref[:, hs]).astype(bf16)

    def ret_k(pair):
        c0 = pair * 2 * RET_HEAD_DIM
        rk = proj(O_RK + c0, O_RK + c0 + 2 * RET_HEAD_DIM)
        for i in range(2):
            hs = slice(c0 + i * RET_HEAD_DIM, c0 + (i + 1) * RET_HEAD_DIM)
            kr = rotate(rk[:, i * RET_HEAD_DIM:(i + 1) * RET_HEAD_DIM]) * (RET_HEAD_DIM ** -0.5)
            if is_prefix:
                kr = kr * valid
            rk_s[:, hs] = kr.astype(bf16)
            kz_s[:, hs] = (kr * zeta_ref[:, hs]).astype(bf16)

    def ret_v(pair):
        c0 = pair * 2 * RET_HEAD_DIM
        rv_s[:, c0:c0 + 2 * RET_HEAD_DIM] = proj(O_RV + c0, O_RV + c0 + 2 * RET_HEAD_DIM).astype(bf16)

    units = [(j, kvh) for j in range(nb) for kvh in range(ATT_KV_HEADS)]
    fillers = [functools.partial(f, pair) for f in (ret_q, ret_k, ret_v) for pair in range(2)]
    scores_q = [att_scores(*units[0])]
    probs_q = []
    for i, unit in enumerate(units):
        if i + 1 < len(units):
            scores_q.append(att_scores(*units[i + 1]))
        if fillers:
            fillers.pop(0)()
        probs_q.append(att_softmax(*unit, scores_q[i]))
        if i >= 1:
            att_values(*units[i - 1], probs_q[i - 1])
            if fillers and i % 2 == 1:
                fillers.pop(0)()
    att_values(*units[-1], probs_q[-1])
    for f in fillers:
        f()

    y_s[0] = (ya_s[...] * _silu(proj(O_AG, O_RQ))).astype(bf16)

    heads = [slice(h * RET_HEAD_DIM, (h + 1) * RET_HEAD_DIM) for h in range(RET_HEADS)]
    blocks = [slice(j * BLOCK, (j + 1) * BLOCK) for j in range(nb)]
    half_w = WIDTH // 2

    def conv_u(half):
        c0 = half * half_w
        u = proj(O_CC + c0, O_CC + c0 + half_w) * proj(O_CX + c0, O_CX + c0 + half_w)
        if is_prefix:
            u = u * valid
        ubuf[CONV_HALO:CONV_HALO + tile, c0:c0 + half_w] = u

    def conv_y(half):
        cs = slice(half * half_w, (half + 1) * half_w)
        conv = (convw_ref[0:1, cs] * ubuf[CONV_HALO - 2:CONV_HALO - 2 + tile, cs]
                + convw_ref[1:2, cs] * ubuf[CONV_HALO - 1:CONV_HALO - 1 + tile, cs]
                + convw_ref[2:3, cs] * ubuf[CONV_HALO:CONV_HALO + tile, cs])
        y_s[2, :, cs] = (proj(O_CB + cs.start, O_CB + cs.stop) * conv
                         * _silu(proj(O_CG + cs.start, O_CG + cs.stop))).astype(bf16)

    def ret_dots(h):
        hs = heads[h]
        return ([_dot_nt(rq_s[rows, hs], rk_s[rows, hs]) for rows in blocks],
                [_dot_tn(kz_s[rows, hs], rv_s[rows, hs]) for rows in blocks])

    def ret_mix(h, scores, chunk_kv):
        hs = heads[h]
        st = state_s[h]
        outs = []
        for j, rows in enumerate(blocks):
            lhs = jnp.concatenate([(scores[j] * decay_ref[h]).astype(bf16), qx_s[rows, hs]], axis=1)
            rhs = jnp.concatenate([rv_s[rows, hs], st.astype(bf16)], axis=0)
            outs.append(_dot(lhs, rhs))
            st = _GAMMA_CHUNK[h] * st + chunk_kv[j]
        state_s[h] = st
        return outs

    def ret_norm(h, outs):
        for rows, o in zip(blocks, outs):
            mu = jnp.mean(o, axis=-1, keepdims=True)
            d = o - mu
            var = jnp.mean(d * d, axis=-1, keepdims=True)
            yr_s[rows, heads[h]] = d * lax.rsqrt(var + GN_EPS)

    fillers = [functools.partial(f, half) for f in (conv_u, conv_y) for half in range(2)]
    dots = ret_dots(0)
    mixed = None
    for h in range(RET_HEADS):
        nxt = ret_dots(h + 1) if h + 1 < RET_HEADS else None
        fillers.pop(0)()
        prev, mixed = mixed, ret_mix(h, *dots)
        if h >= 1:
            ret_norm(h - 1, prev)
        dots = nxt
    ret_norm(RET_HEADS - 1, mixed)
    y_s[1] = (yr_s[...] * _silu(proj(O_RG, O_CB))).astype(bf16)

    for n0 in range(0, D_MODEL, MERGE_CHUNK):
        acc = None
        for g in range(N_BRANCH):
            c0 = O_MERGE + g * D_MODEL + n0
            gate = _sigmoid(proj(c0, c0 + MERGE_CHUNK))
            term = gate * _dot(y_s[g], wbr_ref[g, :, n0:n0 + MERGE_CHUNK])
            acc = term if acc is None else acc + term
        mixed_s[:, n0:n0 + MERGE_CHUNK] = acc.astype(bf16)

    for r0 in range(0, tile, row_group):
        rows = slice(r0, r0 + row_group)
        o = _dot(mixed_s[rows, :], wout_ref[...])
        oms = jnp.mean(o * o, axis=-1, keepdims=True)
        out_ref[rows, :] = x_ref[rows, :] + o * lax.rsqrt(oms + RMS_EPS) * gpost_ref[...]

    ubuf[0:CONV_HALO, :] = ubuf[tile:tile + CONV_HALO, :]
    kbuf[0:BLOCK, :] = kbuf[tile:tile + BLOCK, :]
    vt_s[:, 0:BLOCK] = vt_s[:, tile:tile + BLOCK]
    if is_prefix:
        kst_ref[...] = kbuf[0:BLOCK, :]
        vst_ref[...] = vt_s[:, 0:BLOCK]
        sst_ref[...] = state_s[...]
        ust_ref[...] = ubuf[0:CONV_HALO, :]


def _const_spec(shape, index=None):
    index = (0,) * len(shape) if index is None else index
    return pl.BlockSpec(shape, lambda b, t: index, pipeline_mode=pl.Buffered(1))


def _layer_call(layer, x, cos_tab, sin_tab, params, tables, extra, *, tile, is_prefix):
    B, L, _ = x.shape
    assert L % tile == 0 and tile % BLOCK == 0
    gpre, gpost, win, wbr, wout, convw, sinks = params
    decay, xi_tab, zeta_tab = tables
    nb = tile // BLOCK
    xi_t = jnp.tile(xi_tab, (nb, 1))
    zeta_t = jnp.tile(zeta_tab, (nb, 1))
    bf16, f32 = jnp.bfloat16, jnp.float32
    bias_shape = (ATT_KV_HEADS, BLOCK, ATT_GROUP * BLOCK)

    in_specs = [
        pl.BlockSpec((None, tile, D_MODEL), lambda b, t: (b, t, 0)),
        pl.BlockSpec((tile, RET_HEAD_DIM), lambda b, t: (t, 0)),
        pl.BlockSpec((tile, RET_HEAD_DIM), lambda b, t: (t, 0)),
        _const_spec((None, 1, D_MODEL), (layer, 0, 0)),
        _const_spec((None, 1, D_MODEL), (layer, 0, 0)),
        _const_spec((None, D_MODEL, PROJ_WIDTH), (layer, 0, 0)),
        _const_spec((None, N_BRANCH, WIDTH, D_MODEL), (layer, 0, 0, 0)),
        _const_spec((None, D_MODEL, D_MODEL), (layer, 0, 0)),
        _const_spec((None, CONV_K, WIDTH), (layer, 0, 0)),
        pl.BlockSpec(memory_space=pltpu.SMEM),
        _const_spec((RET_HEADS, BLOCK, BLOCK)),
        _const_spec((tile, WIDTH)), _const_spec((tile, WIDTH)),
    ]
    out_shape = [jax.ShapeDtypeStruct((B, L, D_MODEL), f32)]
    out_specs = [pl.BlockSpec((None, tile, D_MODEL), lambda b, t: (b, t, 0))]
    state_shapes = [((BLOCK, BLOCK), bf16), ((BLOCK, BLOCK), bf16),
                    ((RET_HEADS, BLOCK, BLOCK), f32), ((CONV_HALO, WIDTH), f32)]
    if is_prefix:
        in_specs += [_const_spec((BLOCK, BLOCK)), pl.BlockSpec(memory_space=pltpu.SMEM)]
        for shape, dt in [(bias_shape, f32)] + state_shapes:
            out_shape.append(jax.ShapeDtypeStruct(shape, dt))
            out_specs.append(pl.BlockSpec(shape, lambda b, t, n=len(shape): (0,) * n))
    else:
        in_specs += [_const_spec(bias_shape), _const_spec((BLOCK, BLOCK)), _const_spec((BLOCK, BLOCK)),
                     _const_spec((RET_HEADS, BLOCK, BLOCK)),
                     _const_spec((CONV_HALO, WIDTH))]
    scratch = [
        pltpu.VMEM((tile, D_MODEL), bf16),
        pltpu.VMEM((tile, WIDTH), bf16),
        pltpu.VMEM((tile + BLOCK, BLOCK), bf16),
        pltpu.VMEM((BLOCK, tile + BLOCK), bf16),
        pltpu.VMEM((tile, WIDTH), f32),
        pltpu.VMEM((tile, WIDTH), bf16),
        pltpu.VMEM((tile, WIDTH), bf16),
        pltpu.VMEM((tile, WIDTH), bf16),
        pltpu.VMEM((tile, WIDTH), bf16),
        pltpu.VMEM((tile, WIDTH), bf16),
        pltpu.VMEM((tile, WIDTH), f32),
        pltpu.VMEM((tile + CONV_HALO, WIDTH), f32),
        pltpu.VMEM((RET_HEADS, BLOCK, BLOCK), f32),
        pltpu.VMEM((N_BRANCH, tile, WIDTH), bf16),
        pltpu.VMEM((tile, D_MODEL), bf16),
    ]
    return pl.pallas_call(
        functools.partial(_layer_kernel, layer, tile, is_prefix),
        grid=(B, L // tile),
        in_specs=in_specs,
        out_specs=out_specs,
        out_shape=out_shape,
        scratch_shapes=scratch,
        compiler_params=pltpu.CompilerParams(
            dimension_semantics=("arbitrary", "arbitrary"),
            vmem_limit_bytes=VMEM_LIMIT_BYTES),
        name="hybrid_layer_prefix" if is_prefix else "hybrid_layer_main",
    )(x, cos_tab, sin_tab, gpre, gpost, win, wbr, wout, convw, sinks, decay, xi_t, zeta_t, *extra)


def _t5_bucket_table():
    r = np.arange(BLOCK)[None, :]
    c = np.arange(BLOCK)[:, None]
    n = np.where(c <= r, r - c, BLOCK + r - c)
    nf = np.maximum(n, 1).astype(np.float32)
    large = MAX_EXACT + (np.log(nf / np.float32(MAX_EXACT)) / np.float32(math.log(MAX_DISTANCE / MAX_EXACT))
                         * np.float32(N_BUCKETS - MAX_EXACT)).astype(np.int32)
    large = np.minimum(large, N_BUCKETS - 1)
    return np.where(n < MAX_EXACT, n, large).astype(np.int32)


def _position_tables(n_pos):
    half = RET_HEAD_DIM // 2
    theta = 1.0 / (ROT_BASE ** jnp.linspace(0.0, 1.0, half, dtype=jnp.float32))
    ang = jnp.arange(n_pos).astype(jnp.float32)[:, None] * theta[None, :]
    cos, sin = jnp.cos(ang), jnp.sin(ang)
    return jnp.concatenate([cos, cos], axis=1), jnp.concatenate([-sin, sin], axis=1)


def _retention_tables():
    log_gamma = jnp.asarray(_LOG_GAMMA)
    i = jnp.arange(BLOCK, dtype=jnp.float32)
    diff = i[:, None] - i[None, :]
    decay = jnp.where(diff[None] >= 0, jnp.exp(diff[None] * log_gamma[:, None, None]), 0.0)
    zeta = jnp.exp((BLOCK - 1 - i)[None, :] * log_gamma[:, None])
    xi = jnp.exp((i + 1)[None, :] * log_gamma[:, None])
    expand = lambda a: jnp.repeat(a.T, RET_HEAD_DIM, axis=1)
    return decay, expand(xi), expand(zeta)


def kernel(x, meta_tokens, rel_bias, norm_pre, w_in, conv_w, attn_sinks, w_branch, w_out, norm_post):
    B, S, _ = x.shape
    f32, bf16 = jnp.float32, jnp.bfloat16

    bucket = jnp.asarray(_t5_bucket_table())
    cos_tab, sin_tab = _position_tables(N_META + S)
    cos_pre = jnp.concatenate([jnp.ones((PAD_FRONT, RET_HEAD_DIM), f32), cos_tab[:N_META]], axis=0)
    sin_pre = jnp.concatenate([jnp.zeros((PAD_FRONT, RET_HEAD_DIM), f32), sin_tab[:N_META]], axis=0)
    tables = _retention_tables()
    params = (norm_pre[:, None, :], norm_post[:, None, :], w_in.astype(bf16), w_branch.astype(bf16),
              w_out.astype(bf16), conv_w, attn_sinks)

    h_pre = jnp.concatenate([jnp.zeros((PAD_FRONT, D_MODEL), f32), meta_tokens.astype(f32)], axis=0)[None]
    h = x
    for l in range(DEPTH):
        h_pre, bias_tab, k0, v0, s0, u0 = _layer_call(
            l, h_pre, cos_pre, sin_pre, params, tables, (bucket, rel_bias), tile=BLOCK, is_prefix=True)
        (h,) = _layer_call(
            l, h, cos_tab[N_META:], sin_tab[N_META:], params, tables,
            (bias_tab, k0, v0, s0, u0), tile=TILE, is_prefix=False)
    return h
```

```python
import functools
import math

import numpy as np
import jax
import jax.numpy as jnp
from jax import lax
from jax.experimental import pallas as pl
from jax.experimental.pallas import tpu as pltpu

D_MODEL = 1024
DEPTH = 2
N_META = 16
BLOCK = 128
PAD_FRONT = BLOCK - N_META

ATT_HEADS = 8
ATT_KV_HEADS = 2
ATT_GROUP = ATT_HEADS // ATT_KV_HEADS
ATT_HEAD_DIM = 64
N_BUCKETS = 32
MAX_EXACT = N_BUCKETS // 2
MAX_DISTANCE = 128

RET_HEADS = 4
RET_HEAD_DIM = 128
ROT_BASE = 10000.0

CONV_K = 3
WIDTH = 512
N_BRANCH = 3
RMS_EPS = 1e-6
GN_EPS = 1e-6
NEG_INF = -1e30

_SPLITS = (512, 128, 128, 512, 512, 512, 512, 512, 512, 512, 512, 512, N_BRANCH * D_MODEL)
_OFF = [0] + [int(v) for v in np.cumsum(_SPLITS)]
(O_AQ, O_AK, O_AV, O_AG, O_RQ, O_RK, O_RV, O_RG, O_CB, O_CC, O_CX, O_CG, O_MERGE, PROJ_WIDTH) = _OFF

TILE = 512
CONV_HALO = 8
EDGE_ROWS = 128
MERGE_CHUNK = 512
VMEM_LIMIT_BYTES = 56 * 1024 * 1024

_LOG_GAMMA = np.log1p(-(np.float32(2.0) ** (-5.0 - np.arange(RET_HEADS, dtype=np.float32)))).astype(np.float32)
_GAMMA_CHUNK = [float(v) for v in np.exp(np.float32(BLOCK) * _LOG_GAMMA).astype(np.float32)]


def _silu_from_half(vh):
    return vh * (1.0 + jnp.tanh(vh))


def _twice_sigmoid_from_half(vh):
    return 1.0 + jnp.tanh(vh)


def _dot(a, b):
    return jnp.dot(a, b, preferred_element_type=jnp.float32)


def _dot_nt(a, b):
    return lax.dot_general(a, b, (((1,), (1,)), ((), ())), preferred_element_type=jnp.float32)


def _dot_tn(a, b):
    return lax.dot_general(a, b, (((0,), (0,)), ((), ())), preferred_element_type=jnp.float32)


def _row_groups(tile):
    if tile <= EDGE_ROWS:
        return [(0, tile)]
    return [(0, tile - EDGE_ROWS), (tile - EDGE_ROWS, tile)]


def _layer_kernel(layer, tile, is_prefix, *refs):
    (x_ref, cos_ref, sin_ref, gpre_ref, gpost_ref, win_ref, wbr_ref, wout_ref,
     convw_ref, sink_ref, decay_ref, xi_ref, zeta_ref) = refs[:13]
    refs = refs[13:]
    if not is_prefix:
        xn_ref, refs = refs[0], refs[1:]
    if is_prefix:
        bucket_ref, relb_ref = refs[:2]
        out_ref, bias_ref, kst_ref, vst_ref, sst_ref, ust_ref = refs[2:8]
        refs = refs[8:]
    else:
        bias_ref, k0_ref, v0_ref, s0_ref, u0_ref = refs[:5]
        out_ref = refs[5]
        refs = refs[6:]
    (hb_s, q_s, kbuf, vt_s, ya_s, ga_s, rq_s, qx_s, rk_s, kz_s, rv_s, yr_s, ubuf, state_s, y_s, mixed_s) = refs

    bf16 = jnp.bfloat16
    f32 = jnp.float32
    nb = tile // BLOCK
    post_groups = _row_groups(tile)
    t = pl.program_id(1)

    if is_prefix:
        kbuf[0:BLOCK, :] = jnp.zeros((BLOCK, BLOCK), bf16)
        vt_s[:, 0:BLOCK] = jnp.zeros((BLOCK, BLOCK), bf16)
        state_s[...] = jnp.zeros_like(state_s)
        ubuf[0:CONV_HALO, :] = jnp.zeros((CONV_HALO, WIDTH), f32)
        bkt = bucket_ref[...]
        for hq in range(ATT_HEADS):
            acc = jnp.zeros((BLOCK, BLOCK), f32)
            for b in range(N_BUCKETS):
                acc = jnp.where(bkt == b, relb_ref[b, hq], acc)
            bias_ref[hq // ATT_GROUP, :, (hq % ATT_GROUP) * BLOCK:(hq % ATT_GROUP + 1) * BLOCK] = acc
        row = lax.broadcasted_iota(jnp.int32, (tile, 1), 0)
        valid = (row >= PAD_FRONT).astype(f32)
    else:
        @pl.when(t == 0)
        def _():
            kbuf[0:BLOCK, :] = k0_ref[...]
            vt_s[:, 0:BLOCK] = v0_ref[...]
            state_s[...] = s0_ref[...]
            ubuf[0:CONV_HALO, :] = u0_ref[...]

    kidx = lax.broadcasted_iota(jnp.int32, (BLOCK, ATT_GROUP * BLOCK), 0)
    qidx = lax.broadcasted_iota(jnp.int32, (BLOCK, ATT_GROUP * BLOCK), 1) & (BLOCK - 1)
    from_cur = kidx <= qidx
    if is_prefix:
        first_mask = jnp.where(from_cur & (kidx >= PAD_FRONT), 0.0, NEG_INF).astype(f32)
    else:
        first_mask = jnp.where((t == 0) & jnp.logical_not(from_cur) & (kidx < PAD_FRONT), NEG_INF, 0.0).astype(f32)

    def pre_norm(src_ref, slot, j):
        rows = slice(j * BLOCK, (j + 1) * BLOCK)
        x = src_ref[rows, :]
        ms = jnp.mean(x * x, axis=-1, keepdims=True)
        hb_s[slot, rows, :] = (x * lax.rsqrt(ms + RMS_EPS) * gpre_ref[...]).astype(bf16)

    if is_prefix:
        slot = 0
        pre_norm(x_ref, 0, 0)
    else:
        step = pl.program_id(0) * pl.num_programs(1) + t
        slot = step & 1

        @pl.when(step == 0)
        def _():
            for j in range(nb):
                pre_norm(x_ref, 0, j)

    def proj(a, b):
        return _dot(hb_s[slot], win_ref[:, a:b])

    q_s[...] = (proj(O_AQ, O_AK) * (ATT_HEAD_DIM ** -0.5)).astype(bf16)
    kv = proj(O_AK, O_AG)
    kbuf[BLOCK:BLOCK + tile, :] = kv[:, :BLOCK].astype(bf16)
    vt_s[:, BLOCK:BLOCK + tile] = kv[:, BLOCK:].T.astype(bf16)


    def att_scores(j, kvh):
        rows = slice(j * BLOCK, (j + 1) * BLOCK)
        band = slice(j * BLOCK, (j + 2) * BLOCK)
        qs = jnp.concatenate(
            [q_s[rows, (kvh * ATT_GROUP + g) * ATT_HEAD_DIM:(kvh * ATT_GROUP + g + 1) * ATT_HEAD_DIM]
             for g in range(ATT_GROUP)], axis=0)
        return _dot_nt(kbuf[band, kvh * ATT_HEAD_DIM:(kvh + 1) * ATT_HEAD_DIM], qs)

    def att_softmax(j, kvh, s2):
        s = jnp.where(from_cur, s2[BLOCK:], s2[:BLOCK]) + bias_ref[kvh]
        if j == 0:
            s = s + first_mask
        ps = []
        for g in range(ATT_GROUP):
            sg = s[:, g * BLOCK:(g + 1) * BLOCK]
            sink = sink_ref[layer, kvh * ATT_GROUP + g]
            m = jnp.maximum(jnp.max(sg, axis=0, keepdims=True), sink)
            p = jnp.exp(sg - m)
            denom = jnp.sum(p, axis=0, keepdims=True) + jnp.exp(sink - m)
            ps.append(p * (1.0 / denom))
        pn = jnp.concatenate(ps, axis=1)
        pband = jnp.concatenate([jnp.where(from_cur, 0.0, pn), jnp.where(from_cur, pn, 0.0)], axis=0)
        return pband.astype(bf16)

    def att_values(j, kvh, pband):
        rows = slice(j * BLOCK, (j + 1) * BLOCK)
        band = slice(j * BLOCK, (j + 2) * BLOCK)
        ot = _dot(vt_s[kvh * ATT_HEAD_DIM:(kvh + 1) * ATT_HEAD_DIM, band], pband)
        for pair in range(ATT_GROUP // 2):
            two = jnp.concatenate([ot[:, (2 * pair) * BLOCK:(2 * pair + 1) * BLOCK],
                                   ot[:, (2 * pair + 1) * BLOCK:(2 * pair + 2) * BLOCK]], axis=0)
            h0 = (kvh * ATT_GROUP + 2 * pair) * ATT_HEAD_DIM
            ya_s[rows, h0:h0 + 2 * ATT_HEAD_DIM] = two.T

    cos_t = cos_ref[...]
    sin_t = sin_ref[...]

    def rotate(v):
        return v * cos_t + pltpu.roll(v, RET_HEAD_DIM // 2, 1) * sin_t

    def ret_q(pair):
        c0 = pair * 2 * RET_HEAD_DIM
        rq = proj(O_RQ + c0, O_RQ + c0 + 2 * RET_HEAD_DIM)
        for i in range(2):
            hs = slice(c0 + i * RET_HEAD_DIM, c0 + (i + 1) * RET_HEAD_DIM)
            qr = rotate(rq[:, i * RET_HEAD_DIM:(i + 1) * RET_HEAD_DIM])
            rq_s[:, hs] = qr.astype(bf16)
            qx_s[:, hs] = (qr * xi_ref[:, hs]).astype(bf16)

    def ret_k(pair):
        c0 = pair * 2 * RET_HEAD_DIM
        rk = proj(O_RK + c0, O_RK + c0 + 2 * RET_HEAD_DIM)
        for i in range(2):
            hs = slice(c0 + i * RET_HEAD_DIM, c0 + (i + 1) * RET_HEAD_DIM)
            kr = rotate(rk[:, i * RET_HEAD_DIM:(i + 1) * RET_HEAD_DIM]) * (RET_HEAD_DIM ** -0.5)
            if is_prefix:
                kr = kr * valid
            rk_s[:, hs] = kr.astype(bf16)
            kz_s[:, hs] = (kr * zeta_ref[:, hs]).astype(bf16)

    def ret_v(pair):
        c0 = pair * 2 * RET_HEAD_DIM
        rv_s[:, c0:c0 + 2 * RET_HEAD_DIM] = proj(O_RV + c0, O_RV + c0 + 2 * RET_HEAD_DIM).astype(bf16)

    def att_gate(half):
        c0 = half * 2 * BLOCK
        ga_s[:, c0:c0 + 2 * BLOCK] = _silu_from_half(proj(O_AG + c0, O_AG + c0 + 2 * BLOCK))

    units = [(j, kvh) for j in range(nb) for kvh in range(ATT_KV_HEADS)]
    fillers = [functools.partial(f, pair) for pair in range(2) for f in (ret_q, ret_k, ret_v)]
    fillers += [functools.partial(att_gate, half) for half in range(2)]
    per_unit = -(-len(fillers) // len(units))
    scores_q = []
    probs_q = []
    for i, unit in enumerate(units):
        for f in fillers[i * per_unit:(i + 1) * per_unit]:
            f()
        if i == 0:
            scores_q.append(att_scores(*units[0]))
        if i + 1 < len(units):
            scores_q.append(att_scores(*units[i + 1]))
        probs_q.append(att_softmax(*unit, scores_q[i]))
        if i >= 1:
            att_values(*units[i - 1], probs_q[i - 1])
    att_values(*units[-1], probs_q[-1])
    for f in fillers[len(units) * per_unit:]:
        f()

    y_s[0] = (ya_s[...] * ga_s[...]).astype(bf16)

    heads = [slice(h * RET_HEAD_DIM, (h + 1) * RET_HEAD_DIM) for h in range(RET_HEADS)]
    blocks = [slice(j * BLOCK, (j + 1) * BLOCK) for j in range(nb)]
    half_w = WIDTH // 2

    def conv_u(half):
        c0 = half * half_w
        u = proj(O_CC + c0, O_CC + c0 + half_w) * proj(O_CX + c0, O_CX + c0 + half_w)
        if is_prefix:
            u = u * valid
        ubuf[CONV_HALO:CONV_HALO + tile, c0:c0 + half_w] = u

    def conv_y(half):
        cs = slice(half * half_w, (half + 1) * half_w)
        conv = (convw_ref[0:1, cs] * ubuf[CONV_HALO - 2:CONV_HALO - 2 + tile, cs]
                + convw_ref[1:2, cs] * ubuf[CONV_HALO - 1:CONV_HALO - 1 + tile, cs]
                + convw_ref[2:3, cs] * ubuf[CONV_HALO:CONV_HALO + tile, cs])
        y_s[2, :, cs] = (proj(O_CB + cs.start, O_CB + cs.stop) * conv
                         * _silu_from_half(proj(O_CG + cs.start, O_CG + cs.stop))).astype(bf16)

    def ret_dots(h):
        hs = heads[h]
        return ([_dot_nt(rq_s[rows, hs], rk_s[rows, hs]) for rows in blocks],
                [_dot_tn(kz_s[rows, hs], rv_s[rows, hs]) for rows in blocks])

    def ret_mix(h, scores, chunk_kv):
        hs = heads[h]
        st = state_s[h]
        outs = []
        for j, rows in enumerate(blocks):
            lhs = jnp.concatenate([(scores[j] * decay_ref[h]).astype(bf16), qx_s[rows, hs]], axis=1)
            rhs = jnp.concatenate([rv_s[rows, hs], st.astype(bf16)], axis=0)
            outs.append(_dot(lhs, rhs))
            st = _GAMMA_CHUNK[h] * st + chunk_kv[j]
        state_s[h] = st
        return outs

    def ret_norm(h, outs):
        for rows, o in zip(blocks, outs):
            mu = jnp.mean(o, axis=-1, keepdims=True)
            d = o - mu
            var = jnp.mean(d * d, axis=-1, keepdims=True)
            yr_s[rows, heads[h]] = d * lax.rsqrt(var + GN_EPS)

    fillers = [functools.partial(f, half) for f in (conv_u, conv_y) for half in range(2)]
    dots = ret_dots(0)
    mixed = None
    for h in range(RET_HEADS):
        nxt = ret_dots(h + 1) if h + 1 < RET_HEADS else None
        fillers.pop(0)()
        prev, mixed = mixed, ret_mix(h, *dots)
        if h >= 1:
            ret_norm(h - 1, prev)
        dots = nxt
    ret_norm(RET_HEADS - 1, mixed)
    y_s[1] = (yr_s[...] * _silu_from_half(proj(O_RG, O_CB))).astype(bf16)

    lookahead = [] if is_prefix else list(range(nb))
    for n0 in range(0, D_MODEL, MERGE_CHUNK):
        acc = None
        for g in range(N_BRANCH):
            c0 = O_MERGE + g * D_MODEL + n0
            gate2 = _twice_sigmoid_from_half(proj(c0, c0 + MERGE_CHUNK))
            term = gate2 * _dot(y_s[g], wbr_ref[g, :, n0:n0 + MERGE_CHUNK])
            acc = term if acc is None else acc + term
            if lookahead:
                pre_norm(xn_ref, 1 - slot, lookahead.pop(0))
        mixed_s[:, n0:n0 + MERGE_CHUNK] = acc.astype(bf16)

    for j in lookahead:
        pre_norm(xn_ref, 1 - slot, j)

    for i, (r0, r1) in enumerate(post_groups):
        rows = slice(r0, r1)
        o = _dot(mixed_s[rows, :], wout_ref[...])
        oms = jnp.mean(o * o, axis=-1, keepdims=True)
        out_ref[rows, :] = x_ref[rows, :] + o * lax.rsqrt(oms + RMS_EPS) * gpost_ref[...]

    ubuf[0:CONV_HALO, :] = ubuf[tile:tile + CONV_HALO, :]
    kbuf[0:BLOCK, :] = kbuf[tile:tile + BLOCK, :]
    vt_s[:, 0:BLOCK] = vt_s[:, tile:tile + BLOCK]
    if is_prefix:
        kst_ref[...] = kbuf[0:BLOCK, :]
        vst_ref[...] = vt_s[:, 0:BLOCK]
        sst_ref[...] = state_s[...]
        ust_ref[...] = ubuf[0:CONV_HALO, :]


def _const_spec(shape, index=None):
    index = (0,) * len(shape) if index is None else index
    return pl.BlockSpec(shape, lambda b, t: index, pipeline_mode=pl.Buffered(1))


def _layer_call(layer, x, cos_tab, sin_tab, params, tables, extra, *, tile, is_prefix):
    B, L, _ = x.shape
    assert L % tile == 0 and tile % BLOCK == 0
    gpre, gpost, win, wbr, wout, convw, sinks = params
    decay, xi_tab, zeta_tab = tables
    nb = tile // BLOCK
    xi_t = jnp.tile(xi_tab, (nb, 1))
    zeta_t = jnp.tile(zeta_tab, (nb, 1))
    bf16, f32 = jnp.bfloat16, jnp.float32
    bias_shape = (ATT_KV_HEADS, BLOCK, ATT_GROUP * BLOCK)

    in_specs = [
        pl.BlockSpec((None, tile, D_MODEL), lambda b, t: (b, t, 0)),
        pl.BlockSpec((tile, RET_HEAD_DIM), lambda b, t: (t, 0)),
        pl.BlockSpec((tile, RET_HEAD_DIM), lambda b, t: (t, 0)),
        _const_spec((None, 1, D_MODEL), (layer, 0, 0)),
        _const_spec((None, 1, D_MODEL), (layer, 0, 0)),
        _const_spec((None, D_MODEL, PROJ_WIDTH), (layer, 0, 0)),
        _const_spec((None, N_BRANCH, WIDTH, D_MODEL), (layer, 0, 0, 0)),
        _const_spec((None, D_MODEL, D_MODEL), (layer, 0, 0)),
        _const_spec((None, CONV_K, WIDTH), (layer, 0, 0)),
        pl.BlockSpec(memory_space=pltpu.SMEM),
        _const_spec((RET_HEADS, BLOCK, BLOCK)),
        _const_spec((tile, WIDTH)), _const_spec((tile, WIDTH)),
    ]
    out_shape = [jax.ShapeDtypeStruct((B, L, D_MODEL), f32)]
    out_specs = [pl.BlockSpec((None, tile, D_MODEL), lambda b, t: (b, t, 0))]
    state_shapes = [((BLOCK, BLOCK), bf16), ((BLOCK, BLOCK), bf16),
                    ((RET_HEADS, BLOCK, BLOCK), f32), ((CONV_HALO, WIDTH), f32)]
    if is_prefix:
        in_specs += [_const_spec((BLOCK, BLOCK)), pl.BlockSpec(memory_space=pltpu.SMEM)]
        for shape, dt in [(bias_shape, f32)] + state_shapes:
            out_shape.append(jax.ShapeDtypeStruct(shape, dt))
            out_specs.append(pl.BlockSpec(shape, lambda b, t, n=len(shape): (0,) * n))
    else:
        n_t = L // tile

        def next_tile(b, t):
            nxt = jnp.minimum(b * n_t + t + 1, B * n_t - 1)
            return (nxt // n_t, nxt % n_t, 0)

        extra = (x,) + tuple(extra)
        in_specs += [pl.BlockSpec((None, tile, D_MODEL), next_tile),
                     _const_spec(bias_shape), _const_spec((BLOCK, BLOCK)), _const_spec((BLOCK, BLOCK)),
                     _const_spec((RET_HEADS, BLOCK, BLOCK)),
                     _const_spec((CONV_HALO, WIDTH))]
    scratch = [
        pltpu.VMEM((2, tile, D_MODEL), bf16),
        pltpu.VMEM((tile, WIDTH), bf16),
        pltpu.VMEM((tile + BLOCK, BLOCK), bf16),
        pltpu.VMEM((BLOCK, tile + BLOCK), bf16),
        pltpu.VMEM((tile, WIDTH), f32),
        pltpu.VMEM((tile, WIDTH), f32),
        pltpu.VMEM((tile, WIDTH), bf16),
        pltpu.VMEM((tile, WIDTH), bf16),
        pltpu.VMEM((tile, WIDTH), bf16),
        pltpu.VMEM((tile, WIDTH), bf16),
        pltpu.VMEM((tile, WIDTH), bf16),
        pltpu.VMEM((tile, WIDTH), f32),
        pltpu.VMEM((tile + CONV_HALO, WIDTH), f32),
        pltpu.VMEM((RET_HEADS, BLOCK, BLOCK), f32),
        pltpu.VMEM((N_BRANCH, tile, WIDTH), bf16),
        pltpu.VMEM((tile, D_MODEL), bf16),
    ]
    return pl.pallas_call(
        functools.partial(_layer_kernel, layer, tile, is_prefix),
        grid=(B, L // tile),
        in_specs=in_specs,
        out_specs=out_specs,
        out_shape=out_shape,
        scratch_shapes=scratch,
        compiler_params=pltpu.CompilerParams(
            dimension_semantics=("arbitrary", "arbitrary"),
            vmem_limit_bytes=VMEM_LIMIT_BYTES),
        name="hybrid_layer_prefix" if is_prefix else "hybrid_layer_main",
    )(x, cos_tab, sin_tab, gpre, gpost, win, wbr, wout, convw, sinks, decay, xi_t, zeta_t, *extra)


def _t5_bucket_table():
    r = np.arange(BLOCK)[None, :]
    c = np.arange(BLOCK)[:, None]
    n = np.where(c <= r, r - c, BLOCK + r - c)
    nf = np.maximum(n, 1).astype(np.float32)
    large = MAX_EXACT + (np.log(nf / np.float32(MAX_EXACT)) / np.float32(math.log(MAX_DISTANCE / MAX_EXACT))
                         * np.float32(N_BUCKETS - MAX_EXACT)).astype(np.int32)
    large = np.minimum(large, N_BUCKETS - 1)
    return np.where(n < MAX_EXACT, n, large).astype(np.int32)


def _position_tables(n_pos):
    half = RET_HEAD_DIM // 2
    theta = 1.0 / (ROT_BASE ** jnp.linspace(0.0, 1.0, half, dtype=jnp.float32))
    ang = jnp.arange(n_pos).astype(jnp.float32)[:, None] * theta[None, :]
    cos, sin = jnp.cos(ang), jnp.sin(ang)
    return jnp.concatenate([cos, cos], axis=1), jnp.concatenate([-sin, sin], axis=1)


def _retention_tables():
    log_gamma = jnp.asarray(_LOG_GAMMA)
    i = jnp.arange(BLOCK, dtype=jnp.float32)
    diff = i[:, None] - i[None, :]
    decay = jnp.where(diff[None] >= 0, jnp.exp(diff[None] * log_gamma[:, None, None]), 0.0)
    zeta = jnp.exp((BLOCK - 1 - i)[None, :] * log_gamma[:, None])
    xi = jnp.exp((i + 1)[None, :] * log_gamma[:, None])
    expand = lambda a: jnp.repeat(a.T, RET_HEAD_DIM, axis=1)
    return decay, expand(xi), expand(zeta)


def _half_scaled_weights(w_in, w_branch):
    col = np.ones((PROJ_WIDTH,), np.float32)
    for a, b in ((O_AG, O_RQ), (O_RG, O_CB), (O_CG, O_MERGE), (O_MERGE, PROJ_WIDTH)):
        col[a:b] = 0.5
    return (w_in * jnp.asarray(col)).astype(jnp.bfloat16), (w_branch * 0.5).astype(jnp.bfloat16)


def kernel(x, meta_tokens, rel_bias, norm_pre, w_in, conv_w, attn_sinks, w_branch, w_out, norm_post):
    B, S, _ = x.shape
    f32, bf16 = jnp.float32, jnp.bfloat16

    bucket = jnp.asarray(_t5_bucket_table())
    cos_tab, sin_tab = _position_tables(N_META + S)
    cos_pre = jnp.concatenate([jnp.ones((PAD_FRONT, RET_HEAD_DIM), f32), cos_tab[:N_META]], axis=0)
    sin_pre = jnp.concatenate([jnp.zeros((PAD_FRONT, RET_HEAD_DIM), f32), sin_tab[:N_META]], axis=0)
    tables = _retention_tables()
    params = (norm_pre[:, None, :], norm_post[:, None, :], *_half_scaled_weights(w_in, w_branch),
              w_out.astype(bf16), conv_w, attn_sinks)

    h_pre = jnp.concatenate([jnp.zeros((PAD_FRONT, D_MODEL), f32), meta_tokens.astype(f32)], axis=0)[None]
    h = x
    for l in range(DEPTH):
        h_pre, bias_tab, k0, v0, s0, u0 = _layer_call(
            l, h_pre, cos_pre, sin_pre, params, tables, (bucket, rel_bias), tile=BLOCK, is_prefix=True)
        (h,) = _layer_call(
            l, h, cos_tab[N_META:], sin_tab[N_META:], params, tables,
            (bias_tab, k0, v0, s0, u0), tile=TILE, is_prefix=False)
    return h
```

```python
import functools
import math

import numpy as np
import jax
import jax.numpy as jnp
from jax import lax
from jax.experimental import pallas as pl
from jax.experimental.pallas import tpu as pltpu

D_MODEL = 1024
DEPTH = 2
N_META = 16
BLOCK = 128
PAD_FRONT = BLOCK - N_META

ATT_HEADS = 8
ATT_KV_HEADS = 2
ATT_GROUP = ATT_HEADS // ATT_KV_HEADS
ATT_HEAD_DIM = 64
N_BUCKETS = 32
MAX_EXACT = N_BUCKETS // 2
MAX_DISTANCE = 128

RET_HEADS = 4
RET_HEAD_DIM = 128
ROT_BASE = 10000.0

CONV_K = 3
WIDTH = 512
N_BRANCH = 3
RMS_EPS = 1e-6
GN_EPS = 1e-6
NEG_INF = -1e30

_SPLITS = (512, 128, 128, 512, 512, 512, 512, 512, 512, 512, 512, 512, N_BRANCH * D_MODEL)
_OFF = [0] + [int(v) for v in np.cumsum(_SPLITS)]
(O_AQ, O_AK, O_AV, O_AG, O_RQ, O_RK, O_RV, O_RG, O_CB, O_CC, O_CX, O_CG, O_MERGE, PROJ_WIDTH) = _OFF

TILE = 512
CONV_HALO = 8
EDGE_ROWS = 256
MERGE_CHUNK = 512
VMEM_LIMIT_BYTES = 56 * 1024 * 1024

_LOG_GAMMA = np.log1p(-(np.float32(2.0) ** (-5.0 - np.arange(RET_HEADS, dtype=np.float32)))).astype(np.float32)
_GAMMA_CHUNK = [float(v) for v in np.exp(np.float32(BLOCK) * _LOG_GAMMA).astype(np.float32)]


def _silu_from_half(vh):
    return vh * (1.0 + jnp.tanh(vh))


def _twice_sigmoid_from_half(vh):
    return 1.0 + jnp.tanh(vh)


def _dot(a, b):
    return jnp.dot(a, b, preferred_element_type=jnp.float32)


def _dot_nt(a, b):
    return lax.dot_general(a, b, (((1,), (1,)), ((), ())), preferred_element_type=jnp.float32)


def _dot_tn(a, b):
    return lax.dot_general(a, b, (((0,), (0,)), ((), ())), preferred_element_type=jnp.float32)


def _row_groups(tile):
    if tile <= EDGE_ROWS:
        return [(0, tile)]
    return [(0, tile - EDGE_ROWS), (tile - EDGE_ROWS, tile)]


def _layer_kernel(layer, tile, is_prefix, *refs):
    (x_ref, cos_ref, sin_ref, gpre_ref, gpost_ref, win_ref, wbr_ref, wout_ref,
     convw_ref, sink_ref, decay_ref, xi_ref, zeta_ref) = refs[:13]
    refs = refs[13:]
    if not is_prefix:
        xn_ref, refs = refs[0], refs[1:]
    if is_prefix:
        bucket_ref, relb_ref = refs[:2]
        out_ref, bias_ref, kst_ref, vst_ref, sst_ref, ust_ref = refs[2:8]
        refs = refs[8:]
    else:
        bias_ref, k0_ref, v0_ref, s0_ref, u0_ref = refs[:5]
        out_ref = refs[5]
        refs = refs[6:]
    (hb_s, q_s, kbuf, vt_s, ya_s, ga_s, rq_s, qx_s, rk_s, kz_s, rv_s, yr_s, ubuf, state_s, y_s, mixed_s) = refs

    bf16 = jnp.bfloat16
    f32 = jnp.float32
    nb = tile // BLOCK
    post_groups = _row_groups(tile)
    t = pl.program_id(1)

    if is_prefix:
        kbuf[0:BLOCK, :] = jnp.zeros((BLOCK, BLOCK), bf16)
        vt_s[:, 0:BLOCK] = jnp.zeros((BLOCK, BLOCK), bf16)
        state_s[...] = jnp.zeros_like(state_s)
        ubuf[0:CONV_HALO, :] = jnp.zeros((CONV_HALO, WIDTH), f32)
        bkt = bucket_ref[...]
        for hq in range(ATT_HEADS):
            acc = jnp.zeros((BLOCK, BLOCK), f32)
            for b in range(N_BUCKETS):
                acc = jnp.where(bkt == b, relb_ref[b, hq], acc)
            bias_ref[hq // ATT_GROUP, :, (hq % ATT_GROUP) * BLOCK:(hq % ATT_GROUP + 1) * BLOCK] = acc
        row = lax.broadcasted_iota(jnp.int32, (tile, 1), 0)
        valid = (row >= PAD_FRONT).astype(f32)
    else:
        @pl.when(t == 0)
        def _():
            kbuf[0:BLOCK, :] = k0_ref[...]
            vt_s[:, 0:BLOCK] = v0_ref[...]
            state_s[...] = s0_ref[...]
            ubuf[0:CONV_HALO, :] = u0_ref[...]

    kidx = lax.broadcasted_iota(jnp.int32, (BLOCK, ATT_GROUP * BLOCK), 0)
    qidx = lax.broadcasted_iota(jnp.int32, (BLOCK, ATT_GROUP * BLOCK), 1) & (BLOCK - 1)
    from_cur = kidx <= qidx
    if is_prefix:
        first_mask = jnp.where(from_cur & (kidx >= PAD_FRONT), 0.0, NEG_INF).astype(f32)
    else:
        first_mask = jnp.where((t == 0) & jnp.logical_not(from_cur) & (kidx < PAD_FRONT), NEG_INF, 0.0).astype(f32)

    def pre_norm(src_ref, slot, j):
        rows = slice(j * BLOCK, (j + 1) * BLOCK)
        x = src_ref[rows, :]
        ms = jnp.mean(x * x, axis=-1, keepdims=True)
        hb_s[slot, rows, :] = (x * lax.rsqrt(ms + RMS_EPS) * gpre_ref[...]).astype(bf16)

    def qkv_proj(slot):
        hb = hb_s[slot]
        q_s[...] = (_dot(hb, win_ref[:, O_AQ:O_AK]) * (ATT_HEAD_DIM ** -0.5)).astype(bf16)
        kv = _dot(hb, win_ref[:, O_AK:O_AG])
        kbuf[BLOCK:BLOCK + tile, :] = kv[:, :BLOCK].astype(bf16)
        vt_s[:, BLOCK:BLOCK + tile] = kv[:, BLOCK:].T.astype(bf16)

    if is_prefix:
        slot = 0
        pre_norm(x_ref, 0, 0)
        qkv_proj(0)
    else:
        step = pl.program_id(0) * pl.num_programs(1) + t
        slot = step & 1

        @pl.when(step == 0)
        def _():
            for j in range(nb):
                pre_norm(x_ref, 0, j)
            qkv_proj(0)

    def proj(a, b):
        return _dot(hb_s[slot], win_ref[:, a:b])


    def att_scores(j, kvh):
        rows = slice(j * BLOCK, (j + 1) * BLOCK)
        band = slice(j * BLOCK, (j + 2) * BLOCK)
        qs = jnp.concatenate(
            [q_s[rows, (kvh * ATT_GROUP + g) * ATT_HEAD_DIM:(kvh * ATT_GROUP + g + 1) * ATT_HEAD_DIM]
             for g in range(ATT_GROUP)], axis=0)
        return _dot_nt(kbuf[band, kvh * ATT_HEAD_DIM:(kvh + 1) * ATT_HEAD_DIM], qs)

    def att_softmax(j, kvh, s2):
        s = jnp.where(from_cur, s2[BLOCK:], s2[:BLOCK]) + bias_ref[kvh]
        if j == 0:
            s = s + first_mask
        ps = []
        for g in range(ATT_GROUP):
            sg = s[:, g * BLOCK:(g + 1) * BLOCK]
            sink = sink_ref[layer, kvh * ATT_GROUP + g]
            m = jnp.maximum(jnp.max(sg, axis=0, keepdims=True), sink)
            p = jnp.exp(sg - m)
            denom = jnp.sum(p, axis=0, keepdims=True) + jnp.exp(sink - m)
            ps.append(p * (1.0 / denom))
        pn = jnp.concatenate(ps, axis=1)
        pband = jnp.concatenate([jnp.where(from_cur, 0.0, pn), jnp.where(from_cur, pn, 0.0)], axis=0)
        return pband.astype(bf16)

    def att_values(j, kvh, pband):
        rows = slice(j * BLOCK, (j + 1) * BLOCK)
        band = slice(j * BLOCK, (j + 2) * BLOCK)
        ot = _dot(vt_s[kvh * ATT_HEAD_DIM:(kvh + 1) * ATT_HEAD_DIM, band], pband)
        for pair in range(ATT_GROUP // 2):
            two = jnp.concatenate([ot[:, (2 * pair) * BLOCK:(2 * pair + 1) * BLOCK],
                                   ot[:, (2 * pair + 1) * BLOCK:(2 * pair + 2) * BLOCK]], axis=0)
            h0 = (kvh * ATT_GROUP + 2 * pair) * ATT_HEAD_DIM
            ya_s[rows, h0:h0 + 2 * ATT_HEAD_DIM] = two.T

    cos_t = cos_ref[...]
    sin_t = sin_ref[...]

    def rotate(v):
        return v * cos_t + pltpu.roll(v, RET_HEAD_DIM // 2, 1) * sin_t

    def ret_q(pair):
        c0 = pair * 2 * RET_HEAD_DIM
        rq = proj(O_RQ + c0, O_RQ + c0 + 2 * RET_HEAD_DIM)
        for i in range(2):
            hs = slice(c0 + i * RET_HEAD_DIM, c0 + (i + 1) * RET_HEAD_DIM)
            qr = rotate(rq[:, i * RET_HEAD_DIM:(i + 1) * RET_HEAD_DIM])
            rq_s[:, hs] = qr.astype(bf16)
            qx_s[:, hs] = (qr * xi_ref[:, hs]).astype(bf16)

    def ret_k(pair):
        c0 = pair * 2 * RET_HEAD_DIM
        rk = proj(O_RK + c0, O_RK + c0 + 2 * RET_HEAD_DIM)
        for i in range(2):
            hs = slice(c0 + i * RET_HEAD_DIM, c0 + (i + 1) * RET_HEAD_DIM)
            kr = rotate(rk[:, i * RET_HEAD_DIM:(i + 1) * RET_HEAD_DIM]) * (RET_HEAD_DIM ** -0.5)
            if is_prefix:
                kr = kr * valid
            rk_s[:, hs] = kr.astype(bf16)
            kz_s[:, hs] = (kr * zeta_ref[:, hs]).astype(bf16)

    def ret_v(pair):
        c0 = pair * 2 * RET_HEAD_DIM
        rv_s[:, c0:c0 + 2 * RET_HEAD_DIM] = proj(O_RV + c0, O_RV + c0 + 2 * RET_HEAD_DIM).astype(bf16)

    def att_gate(half):
        c0 = half * 2 * BLOCK
        ga_s[:, c0:c0 + 2 * BLOCK] = _silu_from_half(proj(O_AG + c0, O_AG + c0 + 2 * BLOCK))

    units = [(j, kvh) for j in range(nb) for kvh in range(ATT_KV_HEADS)]
    fillers = [functools.partial(f, pair) for pair in range(2) for f in (ret_q, ret_k, ret_v)]
    fillers += [functools.partial(att_gate, half) for half in range(2)]
    per_unit = -(-len(fillers) // len(units))
    scores_q = []
    probs_q = []
    for i, unit in enumerate(units):
        for f in fillers[i * per_unit:(i + 1) * per_unit]:
            f()
        if i == 0:
            scores_q.append(att_scores(*units[0]))
        if i + 1 < len(units):
            scores_q.append(att_scores(*units[i + 1]))
        probs_q.append(att_softmax(*unit, scores_q[i]))
        if i >= 1:
            att_values(*units[i - 1], probs_q[i - 1])
    att_values(*units[-1], probs_q[-1])
    for f in fillers[len(units) * per_unit:]:
        f()

    y_s[0] = (ya_s[...] * ga_s[...]).astype(bf16)

    heads = [slice(h * RET_HEAD_DIM, (h + 1) * RET_HEAD_DIM) for h in range(RET_HEADS)]
    blocks = [slice(j * BLOCK, (j + 1) * BLOCK) for j in range(nb)]
    half_w = WIDTH // 2

    def conv_u(half):
        c0 = half * half_w
        u = proj(O_CC + c0, O_CC + c0 + half_w) * proj(O_CX + c0, O_CX + c0 + half_w)
        if is_prefix:
            u = u * valid
        ubuf[CONV_HALO:CONV_HALO + tile, c0:c0 + half_w] = u

    def conv_y(half):
        cs = slice(half * half_w, (half + 1) * half_w)
        conv = (convw_ref[0:1, cs] * ubuf[CONV_HALO - 2:CONV_HALO - 2 + tile, cs]
                + convw_ref[1:2, cs] * ubuf[CONV_HALO - 1:CONV_HALO - 1 + tile, cs]
                + convw_ref[2:3, cs] * ubuf[CONV_HALO:CONV_HALO + tile, cs])
        y_s[2, :, cs] = (proj(O_CB + cs.start, O_CB + cs.stop) * conv
                         * _silu_from_half(proj(O_CG + cs.start, O_CG + cs.stop))).astype(bf16)

    def ret_dots(h):
        hs = heads[h]
        return ([_dot_nt(rq_s[rows, hs], rk_s[rows, hs]) for rows in blocks],
                [_dot_tn(kz_s[rows, hs], rv_s[rows, hs]) for rows in blocks])

    def ret_mix(h, scores, chunk_kv):
        hs = heads[h]
        st = state_s[h]
        outs = []
        for j, rows in enumerate(blocks):
            lhs = jnp.concatenate([(scores[j] * decay_ref[h]).astype(bf16), qx_s[rows, hs]], axis=1)
            rhs = jnp.concatenate([rv_s[rows, hs], st.astype(bf16)], axis=0)
            outs.append(_dot(lhs, rhs))
            st = _GAMMA_CHUNK[h] * st + chunk_kv[j]
        state_s[h] = st
        return outs

    def ret_norm(h, outs):
        for rows, o in zip(blocks, outs):
            mu = jnp.mean(o, axis=-1, keepdims=True)
            d = o - mu
            var = jnp.mean(d * d, axis=-1, keepdims=True)
            yr_s[rows, heads[h]] = d * lax.rsqrt(var + GN_EPS)

    fillers = [functools.partial(f, half) for f in (conv_u, conv_y) for half in range(2)]
    dots = ret_dots(0)
    mixed = None
    for h in range(RET_HEADS):
        nxt = ret_dots(h + 1) if h + 1 < RET_HEADS else None
        fillers.pop(0)()
        prev, mixed = mixed, ret_mix(h, *dots)
        if h >= 1:
            ret_norm(h - 1, prev)
        dots = nxt
    ret_norm(RET_HEADS - 1, mixed)
    y_s[1] = (yr_s[...] * _silu_from_half(proj(O_RG, O_CB))).astype(bf16)

    lookahead = [] if is_prefix else list(range(nb))
    for n0 in range(0, D_MODEL, MERGE_CHUNK):
        acc = None
        for g in range(N_BRANCH):
            c0 = O_MERGE + g * D_MODEL + n0
            gate2 = _twice_sigmoid_from_half(proj(c0, c0 + MERGE_CHUNK))
            term = gate2 * _dot(y_s[g], wbr_ref[g, :, n0:n0 + MERGE_CHUNK])
            acc = term if acc is None else acc + term
            if lookahead:
                pre_norm(xn_ref, 1 - slot, lookahead.pop(0))
        mixed_s[:, n0:n0 + MERGE_CHUNK] = acc.astype(bf16)

    for j in lookahead:
        pre_norm(xn_ref, 1 - slot, j)

    def post_norm(rows, o):
        oms = jnp.mean(o * o, axis=-1, keepdims=True)
        out_ref[rows, :] = x_ref[rows, :] + o * lax.rsqrt(oms + RMS_EPS) * gpost_ref[...]

    groups = [slice(r0, r1) for r0, r1 in post_groups]
    outs = [_dot(mixed_s[rows, :], wout_ref[...]) for rows in groups]
    post_norm(groups[0], outs[0])
    ubuf[0:CONV_HALO, :] = ubuf[tile:tile + CONV_HALO, :]
    kbuf[0:BLOCK, :] = kbuf[tile:tile + BLOCK, :]
    vt_s[:, 0:BLOCK] = vt_s[:, tile:tile + BLOCK]
    if not is_prefix:
        qkv_proj(1 - slot)
    for rows, o in zip(groups[1:], outs[1:]):
        post_norm(rows, o)
    if is_prefix:
        kst_ref[...] = kbuf[0:BLOCK, :]
        vst_ref[...] = vt_s[:, 0:BLOCK]
        sst_ref[...] = state_s[...]
        ust_ref[...] = ubuf[0:CONV_HALO, :]


def _const_spec(shape, index=None):
    index = (0,) * len(shape) if index is None else index
    return pl.BlockSpec(shape, lambda b, t: index, pipeline_mode=pl.Buffered(1))


def _layer_call(layer, x, cos_tab, sin_tab, params, tables, extra, *, tile, is_prefix):
    B, L, _ = x.shape
    assert L % tile == 0 and tile % BLOCK == 0
    gpre, gpost, win, wbr, wout, convw, sinks = params
    decay, xi_tab, zeta_tab = tables
    nb = tile // BLOCK
    xi_t = jnp.tile(xi_tab, (nb, 1))
    zeta_t = jnp.tile(zeta_tab, (nb, 1))
    bf16, f32 = jnp.bfloat16, jnp.float32
    bias_shape = (ATT_KV_HEADS, BLOCK, ATT_GROUP * BLOCK)

    in_specs = [
        pl.BlockSpec((None, tile, D_MODEL), lambda b, t: (b, t, 0)),
        pl.BlockSpec((tile, RET_HEAD_DIM), lambda b, t: (t, 0)),
        pl.BlockSpec((tile, RET_HEAD_DIM), lambda b, t: (t, 0)),
        _const_spec((None, 1, D_MODEL), (layer, 0, 0)),
        _const_spec((None, 1, D_MODEL), (layer, 0, 0)),
        _const_spec((None, D_MODEL, PROJ_WIDTH), (layer, 0, 0)),
        _const_spec((None, N_BRANCH, WIDTH, D_MODEL), (layer, 0, 0, 0)),
        _const_spec((None, D_MODEL, D_MODEL), (layer, 0, 0)),
        _const_spec((None, CONV_K, WIDTH), (layer, 0, 0)),
        pl.BlockSpec(memory_space=pltpu.SMEM),
        _const_spec((RET_HEADS, BLOCK, BLOCK)),
        _const_spec((tile, WIDTH)), _const_spec((tile, WIDTH)),
    ]
    out_shape = [jax.ShapeDtypeStruct((B, L, D_MODEL), f32)]
    out_specs = [pl.BlockSpec((None, tile, D_MODEL), lambda b, t: (b, t, 0))]
    state_shapes = [((BLOCK, BLOCK), bf16), ((BLOCK, BLOCK), bf16),
                    ((RET_HEADS, BLOCK, BLOCK), f32), ((CONV_HALO, WIDTH), f32)]
    if is_prefix:
        in_specs += [_const_spec((BLOCK, BLOCK)), pl.BlockSpec(memory_space=pltpu.SMEM)]
        for shape, dt in [(bias_shape, f32)] + state_shapes:
            out_shape.append(jax.ShapeDtypeStruct(shape, dt))
            out_specs.append(pl.BlockSpec(shape, lambda b, t, n=len(shape): (0,) * n))
    else:
        n_t = L // tile

        def next_tile(b, t):
            nxt = jnp.minimum(b * n_t + t + 1, B * n_t - 1)
            return (nxt // n_t, nxt % n_t, 0)

        extra = (x,) + tuple(extra)
        in_specs += [pl.BlockSpec((None, tile, D_MODEL), next_tile),
                     _const_spec(bias_shape), _const_spec((BLOCK, BLOCK)), _const_spec((BLOCK, BLOCK)),
                     _const_spec((RET_HEADS, BLOCK, BLOCK)),
                     _const_spec((CONV_HALO, WIDTH))]
    scratch = [
        pltpu.VMEM((2, tile, D_MODEL), bf16),
        pltpu.VMEM((tile, WIDTH), bf16),
        pltpu.VMEM((tile + BLOCK, BLOCK), bf16),
        pltpu.VMEM((BLOCK, tile + BLOCK), bf16),
        pltpu.VMEM((tile, WIDTH), f32),
        pltpu.VMEM((tile, WIDTH), f32),
        pltpu.VMEM((tile, WIDTH), bf16),
        pltpu.VMEM((tile, WIDTH), bf16),
        pltpu.VMEM((tile, WIDTH), bf16),
        pltpu.VMEM((tile, WIDTH), bf16),
        pltpu.VMEM((tile, WIDTH), bf16),
        pltpu.VMEM((tile, WIDTH), f32),
        pltpu.VMEM((tile + CONV_HALO, WIDTH), f32),
        pltpu.VMEM((RET_HEADS, BLOCK, BLOCK), f32),
        pltpu.VMEM((N_BRANCH, tile, WIDTH), bf16),
        pltpu.VMEM((tile, D_MODEL), bf16),
    ]
    return pl.pallas_call(
        functools.partial(_layer_kernel, layer, tile, is_prefix),
        grid=(B, L // tile),
        in_specs=in_specs,
        out_specs=out_specs,
        out_shape=out_shape,
        scratch_shapes=scratch,
        compiler_params=pltpu.CompilerParams(
            dimension_semantics=("arbitrary", "arbitrary"),
            vmem_limit_bytes=VMEM_LIMIT_BYTES),
        name="hybrid_layer_prefix" if is_prefix else "hybrid_layer_main",
    )(x, cos_tab, sin_tab, gpre, gpost, win, wbr, wout, convw, sinks, decay, xi_t, zeta_t, *extra)


def _t5_bucket_table():
    r = np.arange(BLOCK)[None, :]
    c = np.arange(BLOCK)[:, None]
    n = np.where(c <= r, r - c, BLOCK + r - c)
    nf = np.maximum(n, 1).astype(np.float32)
    large = MAX_EXACT + (np.log(nf / np.float32(MAX_EXACT)) / np.float32(math.log(MAX_DISTANCE / MAX_EXACT))
                         * np.float32(N_BUCKETS - MAX_EXACT)).astype(np.int32)
    large = np.minimum(large, N_BUCKETS - 1)
    return np.where(n < MAX_EXACT, n, large).astype(np.int32)


def _position_tables(n_pos):
    half = RET_HEAD_DIM // 2
    theta = 1.0 / (ROT_BASE ** jnp.linspace(0.0, 1.0, half, dtype=jnp.float32))
    ang = jnp.arange(n_pos).astype(jnp.float32)[:, None] * theta[None, :]
    cos, sin = jnp.cos(ang), jnp.sin(ang)
    return jnp.concatenate([cos, cos], axis=1), jnp.concatenate([-sin, sin], axis=1)


def _retention_tables():
    log_gamma = jnp.asarray(_LOG_GAMMA)
    i = jnp.arange(BLOCK, dtype=jnp.float32)
    diff = i[:, None] - i[None, :]
    decay = jnp.where(diff[None] >= 0, jnp.exp(diff[None] * log_gamma[:, None, None]), 0.0)
    zeta = jnp.exp((BLOCK - 1 - i)[None, :] * log_gamma[:, None])
    xi = jnp.exp((i + 1)[None, :] * log_gamma[:, None])
    expand = lambda a: jnp.repeat(a.T, RET_HEAD_DIM, axis=1)
    return decay, expand(xi), expand(zeta)


def _half_scaled_weights(w_in, w_branch):
    col = np.ones((PROJ_WIDTH,), np.float32)
    for a, b in ((O_AG, O_RQ), (O_RG, O_CB), (O_CG, O_MERGE), (O_MERGE, PROJ_WIDTH)):
        col[a:b] = 0.5
    return (w_in * jnp.asarray(col)).astype(jnp.bfloat16), (w_branch * 0.5).astype(jnp.bfloat16)


def kernel(x, meta_tokens, rel_bias, norm_pre, w_in, conv_w, attn_sinks, w_branch, w_out, norm_post):
    B, S, _ = x.shape
    f32, bf16 = jnp.float32, jnp.bfloat16

    bucket = jnp.asarray(_t5_bucket_table())
    cos_tab, sin_tab = _position_tables(N_META + S)
    cos_pre = jnp.concatenate([jnp.ones((PAD_FRONT, RET_HEAD_DIM), f32), cos_tab[:N_META]], axis=0)
    sin_pre = jnp.concatenate([jnp.zeros((PAD_FRONT, RET_HEAD_DIM), f32), sin_tab[:N_META]], axis=0)
    tables = _retention_tables()
    params = (norm_pre[:, None, :], norm_post[:, None, :], *_half_scaled_weights(w_in, w_branch),
              w_out.astype(bf16), conv_w, attn_sinks)

    h_pre = jnp.concatenate([jnp.zeros((PAD_FRONT, D_MODEL), f32), meta_tokens.astype(f32)], axis=0)[None]
    h = x
    for l in range(DEPTH):
        h_pre, bias_tab, k0, v0, s0, u0 = _layer_call(
            l, h_pre, cos_pre, sin_pre, params, tables, (bucket, rel_bias), tile=BLOCK, is_prefix=True)
        (h,) = _layer_call(
            l, h, cos_tab[N_META:], sin_tab[N_META:], params, tables,
            (bias_tab, k0, v0, s0, u0), tile=TILE, is_prefix=False)
    return h
```

```python
import functools
import math
import types

import numpy as np
import jax
import jax.numpy as jnp
from jax import lax
from jax.experimental import pallas as pl
from jax.experimental.pallas import tpu as pltpu

D_MODEL = 1024
DEPTH = 2
N_META = 16
BLOCK = 128
PAD_FRONT = BLOCK - N_META

ATT_HEADS = 8
ATT_KV_HEADS = 2
ATT_GROUP = ATT_HEADS // ATT_KV_HEADS
ATT_HEAD_DIM = 64
N_BUCKETS = 32
MAX_EXACT = N_BUCKETS // 2
MAX_DISTANCE = 128

RET_HEADS = 4
RET_HEAD_DIM = 128
ROT_BASE = 10000.0

CONV_K = 3
WIDTH = 512
N_BRANCH = 3
RMS_EPS = 1e-6
GN_EPS = 1e-6
NEG_INF = -1e30

_SPLITS = (512, 128, 128, 512, 512, 512, 512, 512, 512, 512, 512, 512, N_BRANCH * D_MODEL)
_OFF = [0] + [int(v) for v in np.cumsum(_SPLITS)]
(O_AQ, O_AK, O_AV, O_AG, O_RQ, O_RK, O_RV, O_RG, O_CB, O_CC, O_CX, O_CG, O_MERGE, PROJ_WIDTH) = _OFF

TILE = 512
CONV_HALO = 8
EDGE_ROWS = 256
MERGE_CHUNK = 512
PREFIX_CHUNK = 1408
N_PREFIX_CHUNKS = PROJ_WIDTH // PREFIX_CHUNK
assert N_PREFIX_CHUNKS * PREFIX_CHUNK == PROJ_WIDTH and PREFIX_CHUNK % BLOCK == 0
VMEM_LIMIT_BYTES = 56 * 1024 * 1024

_LOG_GAMMA = np.log1p(-(np.float32(2.0) ** (-5.0 - np.arange(RET_HEADS, dtype=np.float32)))).astype(np.float32)
_GAMMA_CHUNK = [float(v) for v in np.exp(np.float32(BLOCK) * _LOG_GAMMA).astype(np.float32)]


def _silu_from_half(vh):
    return vh * (1.0 + jnp.tanh(vh))


def _twice_sigmoid_from_half(vh):
    return 1.0 + jnp.tanh(vh)


def _dot(a, b):
    return jnp.dot(a, b, preferred_element_type=jnp.float32)


def _dot_nt(a, b):
    return lax.dot_general(a, b, (((1,), (1,)), ((), ())), preferred_element_type=jnp.float32)


def _dot_tn(a, b):
    return lax.dot_general(a, b, (((0,), (0,)), ((), ())), preferred_element_type=jnp.float32)


def _row_groups(tile):
    if tile <= EDGE_ROWS:
        return [(0, tile)]
    return [(0, tile - EDGE_ROWS), (tile - EDGE_ROWS, tile)]


def _pre_norm_block(src_ref, gpre_ref, hb_s, slot, j):
    rows = slice(j * BLOCK, (j + 1) * BLOCK)
    x = src_ref[rows, :]
    ms = jnp.mean(x * x, axis=-1, keepdims=True)
    hb_s[slot, rows, :] = (x * lax.rsqrt(ms + RMS_EPS) * gpre_ref[...]).astype(jnp.bfloat16)


_BODY_SCRATCH = ("hb_s", "q_s", "kbuf", "vt_s", "ga_s", "rq_s", "qx_s", "rk_s", "kz_s", "rv_s", "yr_s", "ubuf",
                 "state_s", "y_s", "mixed_s")


def _body_scratch(tile):
    bf16, f32 = jnp.bfloat16, jnp.float32
    shapes = {
        "hb_s": ((2, tile, D_MODEL), bf16),
        "q_s": ((tile, WIDTH), bf16),
        "kbuf": ((tile + BLOCK, BLOCK), bf16),
        "vt_s": ((BLOCK, tile + BLOCK), bf16),
        "ga_s": ((tile, WIDTH), f32),
        "rq_s": ((tile, WIDTH), bf16), "qx_s": ((tile, WIDTH), bf16),
        "rk_s": ((tile, WIDTH), bf16), "kz_s": ((tile, WIDTH), bf16), "rv_s": ((tile, WIDTH), bf16),
        "yr_s": ((tile, WIDTH), f32),
        "ubuf": ((tile + CONV_HALO, WIDTH), f32),
        "state_s": ((RET_HEADS, BLOCK, BLOCK), f32),
        "y_s": ((N_BRANCH, tile, WIDTH), bf16),
        "mixed_s": ((tile, D_MODEL), bf16),
    }
    return [pltpu.VMEM(*shapes[name]) for name in _BODY_SCRATCH]


def _layer_body(tile, is_prefix, r):
    bf16 = jnp.bfloat16
    f32 = jnp.float32
    nb = tile // BLOCK
    post_groups = _row_groups(tile)
    t = r.t
    x_ref, out_ref, bias_ref = r.x_ref, r.out_ref, r.bias_ref
    hb_s, q_s, kbuf, vt_s, ga_s = r.hb_s, r.q_s, r.kbuf, r.vt_s, r.ga_s
    rq_s, qx_s, rk_s, kz_s, rv_s, yr_s = r.rq_s, r.qx_s, r.rk_s, r.kz_s, r.rv_s, r.yr_s
    ubuf, state_s, y_s, mixed_s = r.ubuf, r.state_s, r.y_s, r.mixed_s

    if is_prefix:
        kbuf[0:BLOCK, :] = jnp.zeros((BLOCK, BLOCK), bf16)
        vt_s[:, 0:BLOCK] = jnp.zeros((BLOCK, BLOCK), bf16)
        state_s[...] = jnp.zeros_like(state_s)
        ubuf[0:CONV_HALO, :] = jnp.zeros((CONV_HALO, WIDTH), f32)
        bkt = r.bucket_ref[...]
        for hq in range(ATT_HEADS):
            acc = jnp.zeros((BLOCK, BLOCK), f32)
            for b in range(N_BUCKETS):
                acc = jnp.where(bkt == b, r.relb_ref[b, hq], acc)
            bias_ref[hq // ATT_GROUP, :, (hq % ATT_GROUP) * BLOCK:(hq % ATT_GROUP + 1) * BLOCK] = acc
        row = lax.broadcasted_iota(jnp.int32, (tile, 1), 0)
        valid = (row >= PAD_FRONT).astype(f32)
    else:
        @pl.when(t == 0)
        def _():
            kbuf[0:BLOCK, :] = r.k0_ref[...]
            vt_s[:, 0:BLOCK] = r.v0_ref[...]
            state_s[...] = r.s0_ref[...]
            ubuf[0:CONV_HALO, :] = r.u0_ref[...]

    kidx = lax.broadcasted_iota(jnp.int32, (BLOCK, ATT_GROUP * BLOCK), 0)
    qidx = lax.broadcasted_iota(jnp.int32, (BLOCK, ATT_GROUP * BLOCK), 1) & (BLOCK - 1)
    from_cur = kidx <= qidx
    if is_prefix:
        first_mask = jnp.where(from_cur & (kidx >= PAD_FRONT), 0.0, NEG_INF).astype(f32)
    else:
        first_mask = jnp.where((t == 0) & jnp.logical_not(from_cur) & (kidx < PAD_FRONT), NEG_INF, 0.0).astype(f32)

    def qkv_proj(slot):
        q_s[...] = (r.dense(slot, O_AQ, O_AK) * (ATT_HEAD_DIM ** -0.5)).astype(bf16)
        kv = r.dense(slot, O_AK, O_AG)
        kbuf[BLOCK:BLOCK + tile, :] = kv[:, :BLOCK].astype(bf16)
        vt_s[:, BLOCK:BLOCK + tile] = kv[:, BLOCK:].T.astype(bf16)

    if is_prefix:
        slot = 0
        qkv_proj(0)
    else:
        slot = r.step & 1

        @pl.when(r.step == 0)
        def _():
            for j in range(nb):
                _pre_norm_block(x_ref, r.gpre_ref, hb_s, 0, j)
            qkv_proj(0)

    def proj(a, b):
        return r.dense(slot, a, b)


    def att_scores(j, kvh):
        rows = slice(j * BLOCK, (j + 1) * BLOCK)
        band = slice(j * BLOCK, (j + 2) * BLOCK)
        qs = jnp.concatenate(
            [q_s[rows, (kvh * ATT_GROUP + g) * ATT_HEAD_DIM:(kvh * ATT_GROUP + g + 1) * ATT_HEAD_DIM]
             for g in range(ATT_GROUP)], axis=0)
        return _dot_nt(kbuf[band, kvh * ATT_HEAD_DIM:(kvh + 1) * ATT_HEAD_DIM], qs)

    def att_softmax(j, kvh, s2):
        s = jnp.where(from_cur, s2[BLOCK:], s2[:BLOCK]) + bias_ref[kvh]
        if j == 0:
            s = s + first_mask
        ps = []
        for g in range(ATT_GROUP):
            sg = s[:, g * BLOCK:(g + 1) * BLOCK]
            sink = r.sink(kvh * ATT_GROUP + g)
            m = jnp.maximum(jnp.max(sg, axis=0, keepdims=True), sink)
            p = jnp.exp(sg - m)
            denom = jnp.sum(p, axis=0, keepdims=True) + jnp.exp(sink - m)
            ps.append(p * (1.0 / denom))
        pn = jnp.concatenate(ps, axis=1)
        pband = jnp.concatenate([jnp.where(from_cur, 0.0, pn), jnp.where(from_cur, pn, 0.0)], axis=0)
        return pband.astype(bf16)

    def att_values(j, kvh, pband):
        rows = slice(j * BLOCK, (j + 1) * BLOCK)
        band = slice(j * BLOCK, (j + 2) * BLOCK)
        ot = _dot(vt_s[kvh * ATT_HEAD_DIM:(kvh + 1) * ATT_HEAD_DIM, band], pband)
        for pair in range(ATT_GROUP // 2):
            two = jnp.concatenate([ot[:, (2 * pair) * BLOCK:(2 * pair + 1) * BLOCK],
                                   ot[:, (2 * pair + 1) * BLOCK:(2 * pair + 2) * BLOCK]], axis=0)
            cols = slice((kvh * ATT_GROUP + 2 * pair) * ATT_HEAD_DIM, (kvh * ATT_GROUP + 2 * pair + 2) * ATT_HEAD_DIM)
            y_s[0, rows, cols] = (two.T * ga_s[rows, cols]).astype(bf16)

    cos_t = r.cos_ref[...]
    sin_t = r.sin_ref[...]

    def rotate(v):
        return v * cos_t + pltpu.roll(v, RET_HEAD_DIM // 2, 1) * sin_t

    def ret_q(pair):
        c0 = pair * 2 * RET_HEAD_DIM
        rq = proj(O_RQ + c0, O_RQ + c0 + 2 * RET_HEAD_DIM)
        for i in range(2):
            hs = slice(c0 + i * RET_HEAD_DIM, c0 + (i + 1) * RET_HEAD_DIM)
            qr = rotate(rq[:, i * RET_HEAD_DIM:(i + 1) * RET_HEAD_DIM])
            rq_s[:, hs] = qr.astype(bf16)
            qx_s[:, hs] = (qr * r.xi_ref[:, hs]).astype(bf16)

    def ret_k(pair):
        c0 = pair * 2 * RET_HEAD_DIM
        rk = proj(O_RK + c0, O_RK + c0 + 2 * RET_HEAD_DIM)
        for i in range(2):
            hs = slice(c0 + i * RET_HEAD_DIM, c0 + (i + 1) * RET_HEAD_DIM)
            kr = rotate(rk[:, i * RET_HEAD_DIM:(i + 1) * RET_HEAD_DIM]) * (RET_HEAD_DIM ** -0.5)
            if is_prefix:
                kr = kr * valid
            rk_s[:, hs] = kr.astype(bf16)
            kz_s[:, hs] = (kr * r.zeta_ref[:, hs]).astype(bf16)

    def ret_v(pair):
        c0 = pair * 2 * RET_HEAD_DIM
        rv_s[:, c0:c0 + 2 * RET_HEAD_DIM] = proj(O_RV + c0, O_RV + c0 + 2 * RET_HEAD_DIM).astype(bf16)

    def att_gate(half):
        c0 = half * 2 * BLOCK
        ga_s[:, c0:c0 + 2 * BLOCK] = _silu_from_half(proj(O_AG + c0, O_AG + c0 + 2 * BLOCK))

    units = [(j, kvh) for j in range(nb) for kvh in range(ATT_KV_HEADS)]
    fillers = [functools.partial(att_gate, half) for half in range(2)]
    fillers += [functools.partial(f, pair) for pair in range(2) for f in (ret_q, ret_k, ret_v)]
    per_unit = -(-len(fillers) // len(units))
    scores_q = []
    probs_q = []
    for i, unit in enumerate(units):
        for f in fillers[i * per_unit:(i + 1) * per_unit]:
            f()
        if i == 0:
            scores_q.append(att_scores(*units[0]))
        if i + 1 < len(units):
            scores_q.append(att_scores(*units[i + 1]))
        probs_q.append(att_softmax(*unit, scores_q[i]))
        if i >= 1:
            att_values(*units[i - 1], probs_q[i - 1])
    att_values(*units[-1], probs_q[-1])
    for f in fillers[len(units) * per_unit:]:
        f()

    heads = [slice(h * RET_HEAD_DIM, (h + 1) * RET_HEAD_DIM) for h in range(RET_HEADS)]
    blocks = [slice(j * BLOCK, (j + 1) * BLOCK) for j in range(nb)]
    half_w = WIDTH // 2
    convw_ref = r.convw_ref

    def conv_u(half):
        c0 = half * half_w
        u = proj(O_CC + c0, O_CC + c0 + half_w) * proj(O_CX + c0, O_CX + c0 + half_w)
        if is_prefix:
            u = u * valid
        ubuf[CONV_HALO:CONV_HALO + tile, c0:c0 + half_w] = u

    def conv_y(half):
        cs = slice(half * half_w, (half + 1) * half_w)
        conv = (convw_ref[0:1, cs] * ubuf[CONV_HALO - 2:CONV_HALO - 2 + tile, cs]
                + convw_ref[1:2, cs] * ubuf[CONV_HALO - 1:CONV_HALO - 1 + tile, cs]
                + convw_ref[2:3, cs] * ubuf[CONV_HALO:CONV_HALO + tile, cs])
        y_s[2, :, cs] = (proj(O_CB + cs.start, O_CB + cs.stop) * conv
                         * _silu_from_half(proj(O_CG + cs.start, O_CG + cs.stop))).astype(bf16)

    def ret_dots(h):
        hs = heads[h]
        return ([_dot_nt(rq_s[rows, hs], rk_s[rows, hs]) for rows in blocks],
                [_dot_tn(kz_s[rows, hs], rv_s[rows, hs]) for rows in blocks])

    def ret_mix(h, scores, chunk_kv):
        hs = heads[h]
        st = state_s[h]
        outs = []
        for j, rows in enumerate(blocks):
            lhs = jnp.concatenate([(scores[j] * r.decay_ref[h]).astype(bf16), qx_s[rows, hs]], axis=1)
            rhs = jnp.concatenate([rv_s[rows, hs], st.astype(bf16)], axis=0)
            outs.append(_dot(lhs, rhs))
            st = _GAMMA_CHUNK[h] * st + chunk_kv[j]
        state_s[h] = st
        return outs

    def ret_norm(h, outs):
        for rows, o in zip(blocks, outs):
            mu = jnp.mean(o, axis=-1, keepdims=True)
            d = o - mu
            var = jnp.mean(d * d, axis=-1, keepdims=True)
            yr_s[rows, heads[h]] = d * lax.rsqrt(var + GN_EPS)

    fillers = [functools.partial(f, half) for f in (conv_u, conv_y) for half in range(2)]
    dots = ret_dots(0)
    mixed = None
    for h in range(RET_HEADS):
        nxt = ret_dots(h + 1) if h + 1 < RET_HEADS else None
        fillers.pop(0)()
        prev, mixed = mixed, ret_mix(h, *dots)
        if h >= 1:
            ret_norm(h - 1, prev)
        dots = nxt
    ret_norm(RET_HEADS - 1, mixed)
    y_s[1] = (yr_s[...] * _silu_from_half(proj(O_RG, O_CB))).astype(bf16)

    lookahead = [] if is_prefix else list(range(nb))
    for n0 in range(0, D_MODEL, MERGE_CHUNK):
        acc = None
        for g in range(N_BRANCH):
            c0 = O_MERGE + g * D_MODEL + n0
            gate2 = _twice_sigmoid_from_half(proj(c0, c0 + MERGE_CHUNK))
            term = gate2 * _dot(y_s[g], r.wbr_ref[g, :, n0:n0 + MERGE_CHUNK])
            acc = term if acc is None else acc + term
            if lookahead:
                _pre_norm_block(r.xn_ref, r.gpre_ref, hb_s, 1 - slot, lookahead.pop(0))
        mixed_s[:, n0:n0 + MERGE_CHUNK] = acc.astype(bf16)

    for j in lookahead:
        _pre_norm_block(r.xn_ref, r.gpre_ref, hb_s, 1 - slot, j)

    def post_norm(rows, o):
        oms = jnp.mean(o * o, axis=-1, keepdims=True)
        out_ref[rows, :] = x_ref[rows, :] + o * lax.rsqrt(oms + RMS_EPS) * r.gpost_ref[...]

    groups = [slice(r0, r1) for r0, r1 in post_groups]
    outs = [_dot(mixed_s[rows, :], r.wout_ref[...]) for rows in groups]
    post_norm(groups[0], outs[0])
    ubuf[0:CONV_HALO, :] = ubuf[tile:tile + CONV_HALO, :]
    kbuf[0:BLOCK, :] = kbuf[tile:tile + BLOCK, :]
    vt_s[:, 0:BLOCK] = vt_s[:, tile:tile + BLOCK]
    if not is_prefix:
        qkv_proj(1 - slot)
    for rows, o in zip(groups[1:], outs[1:]):
        post_norm(rows, o)
    if is_prefix:
        r.kst_ref[...] = kbuf[0:BLOCK, :]
        r.vst_ref[...] = vt_s[:, 0:BLOCK]
        r.sst_ref[...] = state_s[...]
        r.ust_ref[...] = ubuf[0:CONV_HALO, :]


_MAIN_INPUTS = ("x_ref", "cos_ref", "sin_ref", "gpre_ref", "gpost_ref", "win_ref", "wbr_ref", "wout_ref",
                "convw_ref", "sink_ref", "decay_ref", "xi_ref", "zeta_ref", "xn_ref", "bias_ref", "k0_ref",
                "v0_ref", "s0_ref", "u0_ref")


def _main_kernel(layer, tile, *refs):
    names = _MAIN_INPUTS + ("out_ref",) + _BODY_SCRATCH
    r = types.SimpleNamespace(**dict(zip(names, refs, strict=True)))
    r.t = pl.program_id(1)
    r.step = pl.program_id(0) * pl.num_programs(1) + r.t
    r.dense = lambda slot, a, b: _dot(r.hb_s[slot], r.win_ref[:, a:b])
    r.sink = lambda i: r.sink_ref[layer, i]
    _layer_body(tile, False, r)


_PREFIX_INPUTS = ("x0_ref", "cos_ref", "sin_ref", "gpre_ref", "gpost_ref", "win32_ref", "colscale_ref",
                  "wbr32_ref", "wout32_ref", "convw_ref", "sink_ref", "decay_ref", "xi_ref", "zeta_ref",
                  "bucket_ref", "relb_ref")
_PREFIX_OUTPUTS = ("win_ref", "wbr_ref", "wout_ref", "bias_ref", "kst_ref", "vst_ref", "sst_ref", "ust_ref")


def _prefix_kernel(*refs):
    names = _PREFIX_INPUTS + _PREFIX_OUTPUTS + ("xpre_s", "proj_s") + _BODY_SCRATCH
    r = types.SimpleNamespace(**dict(zip(names, refs, strict=True)))
    layer = pl.program_id(0)
    chunk = pl.program_id(1)
    bf16 = jnp.bfloat16

    @pl.when((layer == 0) & (chunk == 0))
    def _():
        r.xpre_s[...] = r.x0_ref[...]

    @pl.when(chunk == 0)
    def _():
        _pre_norm_block(r.xpre_s, r.gpre_ref, r.hb_s, 0, 0)
        r.wbr_ref[...] = (r.wbr32_ref[...] * 0.5).astype(bf16)
        r.wout_ref[...] = r.wout32_ref[...].astype(bf16)

    w = (r.win32_ref[...] * r.colscale_ref[...]).astype(bf16)
    r.win_ref[...] = w
    r.proj_s[chunk] = _dot(r.hb_s[0], w)

    def dense(slot, a, b):
        pieces = []
        while a < b:
            ci = a // PREFIX_CHUNK
            end = min(b, (ci + 1) * PREFIX_CHUNK)
            pieces.append(r.proj_s[ci, :, a - ci * PREFIX_CHUNK:end - ci * PREFIX_CHUNK])
            a = end
        return pieces[0] if len(pieces) == 1 else jnp.concatenate(pieces, axis=1)

    @pl.when(chunk == N_PREFIX_CHUNKS - 1)
    def _():
        r.t = 0
        r.x_ref = r.out_ref = r.xpre_s
        r.dense = dense
        r.sink = lambda i: r.sink_ref[layer, i]
        _layer_body(BLOCK, True, r)


def _const_spec(shape, index=None):
    index = (0,) * len(shape) if index is None else index
    return pl.BlockSpec(shape, lambda *_: index, pipeline_mode=pl.Buffered(1))


def _layer_slab(shape, extra=None, **kwargs):
    def index(layer, chunk):
        tail = (0,) * len(shape) if extra is None else extra(chunk)
        return (layer,) + tail
    return pl.BlockSpec((None,) + tuple(shape), index, **kwargs)


_STATE_SHAPES = (((BLOCK, BLOCK), jnp.bfloat16), ((BLOCK, BLOCK), jnp.bfloat16),
                 ((RET_HEADS, BLOCK, BLOCK), jnp.float32), ((CONV_HALO, WIDTH), jnp.float32))
_BIAS_SHAPE = (ATT_KV_HEADS, BLOCK, ATT_GROUP * BLOCK)


def _prefix_call(x0, cos_pre, sin_pre, gpre, gpost, w_in, colscale, w_branch, w_out, conv_w, sinks, tables,
                 bucket, rel_bias):
    decay, xi_tab, zeta_tab = tables
    bf16, f32 = jnp.bfloat16, jnp.float32
    whole = lambda shape: pl.BlockSpec(shape, lambda layer, chunk: (0,) * len(shape))
    chunk_cols = lambda chunk: (0, chunk)
    in_specs = [
        whole((BLOCK, D_MODEL)), whole((BLOCK, RET_HEAD_DIM)), whole((BLOCK, RET_HEAD_DIM)),
        _layer_slab((1, D_MODEL)), _layer_slab((1, D_MODEL)),
        _layer_slab((D_MODEL, PREFIX_CHUNK), chunk_cols),
        pl.BlockSpec((1, PREFIX_CHUNK), lambda layer, chunk: (0, chunk)),
        _layer_slab((N_BRANCH, WIDTH, D_MODEL), pipeline_mode=pl.Buffered(1)),
        _layer_slab((D_MODEL, D_MODEL), pipeline_mode=pl.Buffered(1)),
        _layer_slab((CONV_K, WIDTH)),
        pl.BlockSpec(memory_space=pltpu.SMEM),
        whole((RET_HEADS, BLOCK, BLOCK)), whole((BLOCK, WIDTH)), whole((BLOCK, WIDTH)),
        whole((BLOCK, BLOCK)), pl.BlockSpec(memory_space=pltpu.SMEM),
    ]
    out_shape = [jax.ShapeDtypeStruct((DEPTH, D_MODEL, PROJ_WIDTH), bf16),
                 jax.ShapeDtypeStruct((DEPTH, N_BRANCH, WIDTH, D_MODEL), bf16),
                 jax.ShapeDtypeStruct((DEPTH, D_MODEL, D_MODEL), bf16),
                 jax.ShapeDtypeStruct((DEPTH,) + _BIAS_SHAPE, f32)]
    out_specs = [_layer_slab((D_MODEL, PREFIX_CHUNK), chunk_cols),
                 _layer_slab((N_BRANCH, WIDTH, D_MODEL)), _layer_slab((D_MODEL, D_MODEL)),
                 _layer_slab(_BIAS_SHAPE)]
    for shape, dt in _STATE_SHAPES:
        out_shape.append(jax.ShapeDtypeStruct((DEPTH,) + shape, dt))
        out_specs.append(_layer_slab(shape))
    scratch = [pltpu.VMEM((BLOCK, D_MODEL), f32),
               pltpu.VMEM((N_PREFIX_CHUNKS, BLOCK, PREFIX_CHUNK), f32)]
    return pl.pallas_call(
        _prefix_kernel,
        grid=(DEPTH, N_PREFIX_CHUNKS),
        in_specs=in_specs,
        out_specs=out_specs,
        out_shape=out_shape,
        scratch_shapes=scratch + _body_scratch(BLOCK),
        compiler_params=pltpu.CompilerParams(
            dimension_semantics=("arbitrary", "arbitrary"),
            vmem_limit_bytes=VMEM_LIMIT_BYTES),
        name="hybrid_prefix",
    )(x0, cos_pre, sin_pre, gpre, gpost, w_in, colscale, w_branch, w_out, conv_w, sinks, decay, xi_tab, zeta_tab,
      bucket, rel_bias)


def _main_call(layer, x, cos_tab, sin_tab, params, tables, state, *, tile):
    B, L, _ = x.shape
    assert L % tile == 0 and tile % BLOCK == 0
    gpre, gpost, win, wbr, wout, convw, sinks = params
    decay, xi_tab, zeta_tab = tables
    nb = tile // BLOCK
    n_t = L // tile
    xi_t = jnp.tile(xi_tab, (nb, 1))
    zeta_t = jnp.tile(zeta_tab, (nb, 1))

    def next_tile(b, t):
        nxt = jnp.minimum(b * n_t + t + 1, B * n_t - 1)
        return (nxt // n_t, nxt % n_t, 0)

    slab = lambda shape: _const_spec((None,) + tuple(shape), (layer,) + (0,) * len(shape))
    in_specs = [
        pl.BlockSpec((None, tile, D_MODEL), lambda b, t: (b, t, 0)),
        pl.BlockSpec((tile, RET_HEAD_DIM), lambda b, t: (t, 0)),
        pl.BlockSpec((tile, RET_HEAD_DIM), lambda b, t: (t, 0)),
        slab((1, D_MODEL)), slab((1, D_MODEL)),
        slab((D_MODEL, PROJ_WIDTH)), slab((N_BRANCH, WIDTH, D_MODEL)), slab((D_MODEL, D_MODEL)),
        slab((CONV_K, WIDTH)),
        pl.BlockSpec(memory_space=pltpu.SMEM),
        _const_spec((RET_HEADS, BLOCK, BLOCK)), _const_spec((tile, WIDTH)), _const_spec((tile, WIDTH)),
        pl.BlockSpec((None, tile, D_MODEL), next_tile),
        slab(_BIAS_SHAPE),
    ] + [slab(shape) for shape, _ in _STATE_SHAPES]
    return pl.pallas_call(
        functools.partial(_main_kernel, layer, tile),
        grid=(B, n_t),
        in_specs=in_specs,
        out_specs=pl.BlockSpec((None, tile, D_MODEL), lambda b, t: (b, t, 0)),
        out_shape=jax.ShapeDtypeStruct((B, L, D_MODEL), jnp.float32),
        scratch_shapes=_body_scratch(tile),
        compiler_params=pltpu.CompilerParams(
            dimension_semantics=("arbitrary", "arbitrary"),
            vmem_limit_bytes=VMEM_LIMIT_BYTES),
        name="hybrid_layer_main",
    )(x, cos_tab, sin_tab, gpre, gpost, win, wbr, wout, convw, sinks, decay, xi_t, zeta_t, x, *state)


def _t5_bucket_table():
    r = np.arange(BLOCK)[None, :]
    c = np.arange(BLOCK)[:, None]
    n = np.where(c <= r, r - c, BLOCK + r - c)
    nf = np.maximum(n, 1).astype(np.float32)
    large = MAX_EXACT + (np.log(nf / np.float32(MAX_EXACT)) / np.float32(math.log(MAX_DISTANCE / MAX_EXACT))
                         * np.float32(N_BUCKETS - MAX_EXACT)).astype(np.int32)
    large = np.minimum(large, N_BUCKETS - 1)
    return np.where(n < MAX_EXACT, n, large).astype(np.int32)


def _position_tables(n_pos):
    half = RET_HEAD_DIM // 2
    theta = 1.0 / (ROT_BASE ** jnp.linspace(0.0, 1.0, half, dtype=jnp.float32))
    ang = jnp.arange(n_pos).astype(jnp.float32)[:, None] * theta[None, :]
    cos, sin = jnp.cos(ang), jnp.sin(ang)
    return jnp.concatenate([cos, cos], axis=1), jnp.concatenate([-sin, sin], axis=1)


def _retention_tables():
    log_gamma = jnp.asarray(_LOG_GAMMA)
    i = jnp.arange(BLOCK, dtype=jnp.float32)
    diff = i[:, None] - i[None, :]
    decay = jnp.where(diff[None] >= 0, jnp.exp(diff[None] * log_gamma[:, None, None]), 0.0)
    zeta = jnp.exp((BLOCK - 1 - i)[None, :] * log_gamma[:, None])
    xi = jnp.exp((i + 1)[None, :] * log_gamma[:, None])
    expand = lambda a: jnp.repeat(a.T, RET_HEAD_DIM, axis=1)
    return decay, expand(xi), expand(zeta)


def _gate_half_scale():
    col = np.ones((1, PROJ_WIDTH), np.float32)
    for a, b in ((O_AG, O_RQ), (O_RG, O_CB), (O_CG, O_MERGE), (O_MERGE, PROJ_WIDTH)):
        col[:, a:b] = 0.5
    return col


def kernel(x, meta_tokens, rel_bias, norm_pre, w_in, conv_w, attn_sinks, w_branch, w_out, norm_post):
    B, S, _ = x.shape
    f32 = jnp.float32

    cos_tab, sin_tab = _position_tables(N_META + S)
    cos_pre = jnp.concatenate([jnp.ones((PAD_FRONT, RET_HEAD_DIM), f32), cos_tab[:N_META]], axis=0)
    sin_pre = jnp.concatenate([jnp.zeros((PAD_FRONT, RET_HEAD_DIM), f32), sin_tab[:N_META]], axis=0)
    tables = _retention_tables()
    gpre, gpost = norm_pre[:, None, :], norm_post[:, None, :]

    x0 = jnp.concatenate([jnp.zeros((PAD_FRONT, D_MODEL), f32), meta_tokens.astype(f32)], axis=0)
    win, wbr, wout, *state = _prefix_call(
        x0, cos_pre, sin_pre, gpre, gpost, w_in, jnp.asarray(_gate_half_scale()), w_branch, w_out, conv_w,
        attn_sinks, tables, jnp.asarray(_t5_bucket_table()), rel_bias)
    params = (gpre, gpost, win, wbr, wout, conv_w, attn_sinks)

    h = x
    for l in range(DEPTH):
        h = _main_call(l, h, cos_tab[N_META:], sin_tab[N_META:], params, tables, state, tile=TILE)
    return h
```
